```python
import math
import numpy as np
import jax
import jax.numpy as jnp
from jax import lax

D_MODEL = 1024
BATCH = 16
SEQ = 2048
DEPTH = 4

CTX_LEN = 256
GRID_W = 64
HEAD_DIM = 64
ROPE_FREQS = HEAD_DIM // 4
ROPE_THETA = 10000.0
Q_BLOCK = 128
EPS = 1e-6
MOD_SCALE = 0.02

A_HEADS = 4
A_VDIM = 2 * HEAD_DIM
B_HEADS = 4
B_DK = 64
B_DV = 128
GATE_RANK = 16
GATE_TAU = 16.0
GLA_CHUNK = 64
EVEN_SPLITS = (A_HEADS * 2 * HEAD_DIM, A_HEADS * 2 * HEAD_DIM, A_HEADS * A_VDIM,
               B_HEADS * B_DK, B_HEADS * B_DK, B_HEADS * B_DV, B_HEADS * B_DV,
               GATE_RANK, GATE_RANK)
EVEN_IN = sum(EVEN_SPLITS)
EVEN_MIX = A_HEADS * A_VDIM + B_HEADS * B_DV

C_Q_HEADS = 16
C_KV_HEADS = 4
C_GROUP = C_Q_HEADS // C_KV_HEADS
ODD_SPLITS = (C_Q_HEADS * HEAD_DIM, C_KV_HEADS * HEAD_DIM, C_KV_HEADS * HEAD_DIM)
ODD_IN = sum(ODD_SPLITS)
C_WIDTH = C_Q_HEADS * HEAD_DIM

PEER_HEADS = 8
PEER_NKEYS = 128
PEER_EXPERTS = PEER_NKEYS * PEER_NKEYS
PEER_DK = 128
PEER_TOPK = 16
PEER_CHUNK = 128
PEER_V_SCALE = PEER_HEADS ** -0.5

N_EVEN = (DEPTH + 1) // 2
N_ODD = DEPTH // 2

kernel_name = 'hybrid_diffattn_gla_gqa_peer_dit'


def rms_norm(x, g):
    xf = x.astype(jnp.float32)
    y = xf * lax.rsqrt(jnp.mean(xf * xf, axis=-1, keepdims=True) + EPS)
    return (y * g.astype(jnp.float32)).astype(x.dtype)


def modulate(h, shift, scale):
    return h * (1.0 + scale) + shift


def _split(p, sizes):
    return jnp.split(p, np.cumsum(sizes)[:-1].tolist(), axis=-1)


def _heads(t, n, d):
    b, l, _ = t.shape
    return t.reshape(b, l, n, d).transpose(0, 2, 1, 3)


def _merge(t):
    b, n, l, d = t.shape
    return t.transpose(0, 2, 1, 3).reshape(b, l, n * d)


def _flip(t):
    return jnp.flip(t, axis=2)


def axial_rope_tables(n_tokens):
    rows = n_tokens // GRID_W
    row = jnp.repeat(jnp.arange(rows, dtype=jnp.float32), GRID_W)
    col = jnp.tile(jnp.arange(GRID_W, dtype=jnp.float32), rows)
    inv_freq = ROPE_THETA ** (-jnp.arange(ROPE_FREQS, dtype=jnp.float32) / ROPE_FREQS)
    ang = jnp.stack([row, col], axis=-1)[:, :, None] * inv_freq
    return jnp.cos(ang), jnp.sin(ang)


def apply_axial_rope(t, cos, sin):
    ts = t.reshape(t.shape[:-1] + (2, 2, ROPE_FREQS))
    t1, t2 = ts[..., 0, :], ts[..., 1, :]
    cos = cos.astype(t.dtype)
    sin = sin.astype(t.dtype)
    return jnp.stack([t1 * cos - t2 * sin, t2 * cos + t1 * sin], axis=-2).reshape(t.shape)


def sweep_query_blocks(fn, q, axis):
    nb = q.shape[axis] // Q_BLOCK
    qb = q.reshape(q.shape[:axis] + (nb, Q_BLOCK) + q.shape[axis + 1:])
    ob = lax.map(fn, jnp.moveaxis(qb, axis, 0))
    ob = jnp.moveaxis(ob, 0, axis)
    return ob.reshape(ob.shape[:axis] + (nb * Q_BLOCK,) + ob.shape[axis + 2:])


def diff_attend(q, k, v, lam):
    s = jnp.einsum('bhqmd,bhmkd->bhmqk', q, k).astype(jnp.float32) * HEAD_DIM ** -0.5
    p = jax.nn.softmax(s, axis=-1)
    w = p[:, :, 0] - lam * p[:, :, 1]
    return jnp.einsum('bhqk,bhkv->bhqv', w.astype(v.dtype), v)


def gqa_attend(q, k, v):
    s = jnp.einsum('bkgqd,bkld->bkgql', q, k).astype(jnp.float32) * HEAD_DIM ** -0.5
    p = jax.nn.softmax(s, axis=-1)
    return jnp.einsum('bkgql,bkld->bkgqd', p.astype(v.dtype), v)


def gla_chunked(q, k, v, log_a, h0):
    b, h, l, dk = q.shape
    dv = v.shape[-1]
    nc = l // GLA_CHUNK
    q, k, log_a = (t.reshape(b, h, nc, GLA_CHUNK, dk) for t in (q, k, log_a))
    v = v.reshape(b, h, nc, GLA_CHUNK, dv)
    cum = jnp.cumsum(log_a, axis=3)
    last = cum[:, :, :, -1:, :]
    q_dec = q * jnp.exp(cum)
    k_inv = k * jnp.exp(-cum)
    k_end = k * jnp.exp(last - cum)
    tri = jnp.tril(jnp.ones((GLA_CHUNK, GLA_CHUNK), dtype=bool))
    a = jnp.where(tri, jnp.einsum('bhncd,bhnsd->bhncs', q_dec, k_inv), 0.0)
    o_intra = jnp.einsum('bhncs,bhnsv->bhncv', a, v)
    d_state = jnp.einsum('bhncd,bhncv->bhndv', k_end, v)
    decay = jnp.exp(last[:, :, :, 0, :])

    def step(state, inp):
        g, ds = inp
        return g[..., None] * state + ds, state

    h_fin, h_in = lax.scan(step, h0, (jnp.moveaxis(decay, 2, 0), jnp.moveaxis(d_state, 2, 0)))
    o_inter = jnp.einsum('bhncd,bhndv->bhncv', q_dec, jnp.moveaxis(h_in, 0, 2))
    return (o_intra + o_inter).reshape(b, h, l, dv), h_fin


def even_mixer(a_x, a_c, cos, sin, w_in, w_out, q_gain, k_gain, lam_vec, subln,
               w_af, b_af, w_ab, b_ab, gla_gain, lam_init, need_ctx):
    b = a_x.shape[0]
    px = _split(a_x @ w_in, EVEN_SPLITS)
    pc = _split(a_c @ w_in, EVEN_SPLITS)
    lv = lam_vec.astype(jnp.float32)
    lam = jnp.exp(jnp.sum(lv[0] * lv[1])) - jnp.exp(jnp.sum(lv[2] * lv[3])) + lam_init

    def diff_qkv(p, rope):
        q = rms_norm(_heads(p[0], 2 * A_HEADS, HEAD_DIM), q_gain)
        k = rms_norm(_heads(p[1], 2 * A_HEADS, HEAD_DIM), k_gain)
        if rope:
            q = apply_axial_rope(q, cos, sin)
            k = apply_axial_rope(k, cos, sin)
        l = q.shape[2]
        q = q.reshape(b, A_HEADS, 2, l, HEAD_DIM).transpose(0, 1, 3, 2, 4)
        k = k.reshape(b, A_HEADS, 2, l, HEAD_DIM)
        v = _heads(p[2], A_HEADS, A_VDIM)
        return q, k, v

    q_x, k_x, v_x = diff_qkv(px, True)
    q_c, k_c, v_c = diff_qkv(pc, False)
    k_all = jnp.concatenate([k_c, k_x], axis=3)
    v_all = jnp.concatenate([v_c, v_x], axis=2)

    def diff_post(o):
        return _merge(rms_norm(o, subln) * (1.0 - lam_init))

    d_x = diff_post(sweep_query_blocks(lambda qb: diff_attend(qb, k_all, v_all, lam), q_x, 2))

    def gla_in(p):
        q = _heads(p[3], B_HEADS, B_DK).astype(jnp.float32) * B_DK ** -0.5
        k = _heads(p[4], B_HEADS, B_DK).astype(jnp.float32)
        v = _heads(p[5], B_HEADS, B_DV).astype(jnp.float32)
        la_f = _heads(jax.nn.log_sigmoid((p[7] @ w_af + b_af).astype(jnp.float32)) / GATE_TAU, B_HEADS, B_DK)
        la_b = _heads(jax.nn.log_sigmoid((p[8] @ w_ab + b_ab).astype(jnp.float32)) / GATE_TAU, B_HEADS, B_DK)
        return q, k, v, la_f, la_b

    gq_x, gk_x, gv_x, gf_x, gb_x = gla_in(px)
    gq_c, gk_c, gv_c, gf_c, gb_c = gla_in(pc)
    zeros = jnp.zeros((b, B_HEADS, B_DK, B_DV), jnp.float32)
    oc_f, hc_f = gla_chunked(gq_c, gk_c, gv_c, gf_c, zeros)
    oc_b, hc_b = gla_chunked(_flip(gq_c), _flip(gk_c), _flip(gv_c), _flip(gb_c), zeros)
    ox_f, _ = gla_chunked(gq_x, gk_x, gv_x, gf_x, hc_f)
    ox_b, _ = gla_chunked(_flip(gq_x), _flip(gk_x), _flip(gv_x), _flip(gb_x), hc_b)

    def gla_post(o, r):
        return _merge(rms_norm(o, gla_gain).astype(r.dtype)) * jax.nn.silu(r)

    g_x = gla_post(ox_f + _flip(ox_b), px[6])
    y_x = jnp.concatenate([d_x, g_x], axis=-1) @ w_out
    y_c = None
    if need_ctx:
        d_c = diff_post(diff_attend(q_c, k_c, v_c, lam))
        g_c = gla_post(oc_f + _flip(oc_b), pc[6])
        y_c = jnp.concatenate([d_c, g_c], axis=-1) @ w_out
    return y_x, y_c


def odd_mixer(a_x, a_c, cos, sin, w_in, w_out, q_gain, k_gain, need_ctx):
    q_cols = ODD_SPLITS[0]

    def qkv(a, with_q, rope):
        if with_q:
            pq, pk, pv = _split(a @ w_in, ODD_SPLITS)
        else:
            pk, pv = _split(a @ w_in[:, q_cols:], ODD_SPLITS[1:])
        k = rms_norm(_heads(pk, C_KV_HEADS, HEAD_DIM), k_gain)
        v = _heads(pv, C_KV_HEADS, HEAD_DIM)
        if rope:
            k = apply_axial_rope(k, cos, sin)
        q = None
        if with_q:
            q = rms_norm(_heads(pq, C_Q_HEADS, HEAD_DIM), q_gain)
            if rope:
                q = apply_axial_rope(q, cos, sin)
            bq, _, l, _ = q.shape
            q = q.reshape(bq, C_KV_HEADS, C_GROUP, l, HEAD_DIM)
        return q, k, v

    q_x, k_x, v_x = qkv(a_x, True, True)
    q_c, k_c, v_c = qkv(a_c, need_ctx, False)
    k_all = jnp.concatenate([k_c, k_x], axis=2)
    v_all = jnp.concatenate([v_c, v_x], axis=2)

    def post(o):
        bo, _, _, l, _ = o.shape
        return _merge(o.reshape(bo, C_Q_HEADS, l, HEAD_DIM)) @ w_out

    y_x = post(sweep_query_blocks(lambda qb: gqa_attend(qb, k_all, v_all), q_x, 3))
    y_c = post(gqa_attend(q_c, k_c, v_c)) if need_ctx else None
    return y_x, y_c


def peer_ffn(h, w_q, sub_keys, u_tab, v_tab):
    b, l, d = h.shape
    half = PEER_DK // 2

    def chunk(xc):
        q = (xc @ w_q).reshape(PEER_CHUNK, PEER_HEADS, PEER_DK)
        s1 = jnp.einsum('thd,hnd->thn', q[..., :half], sub_keys[:, 0]).astype(jnp.float32)
        s2 = jnp.einsum('thd,hnd->thn', q[..., half:], sub_keys[:, 1]).astype(jnp.float32)
        v1, i1 = lax.top_k(s1, PEER_TOPK)
        v2, i2 = lax.top_k(s2, PEER_TOPK)
        cand = (v1[..., :, None] + v2[..., None, :]).reshape(PEER_CHUNK, PEER_HEADS, PEER_TOPK * PEER_TOPK)
        cidx = (i1[..., :, None] * PEER_NKEYS + i2[..., None, :]).reshape(PEER_CHUNK, PEER_HEADS, PEER_TOPK * PEER_TOPK)
        best, j = lax.top_k(cand, PEER_TOPK)
        idx = jnp.take_along_axis(cidx, j, axis=-1)
        g = jax.nn.softmax(best, axis=-1)
        act = jax.nn.gelu(jnp.einsum('thkd,td->thk', u_tab[idx], xc).astype(jnp.float32))
        w = (g * act).astype(xc.dtype)
        return jnp.einsum('thk,thkd->td', w, v_tab[idx])

    out = lax.map(chunk, h.reshape(-1, PEER_CHUNK, d))
    return out.reshape(b, l, d)


def setup_inputs(seed: int = 0) -> dict:
    key = jax.random.key(seed)
    ks = jax.random.split(key, 30)

    def nrm(k, shape, s):
        return jax.random.normal(k, shape, jnp.float32) * s

    D = D_MODEL
    return {
        'x': nrm(ks[0], (BATCH, SEQ, D), 1.0),
        'c': nrm(ks[1], (BATCH, D), 1.0),
        'ctx': nrm(ks[2], (BATCH, CTX_LEN, D), 1.0),
        'c_ctx': nrm(ks[3], (D,), 1.0),
        'w_mod': nrm(ks[4], (DEPTH, D, 6 * D), MOD_SCALE),
        'b_mod': nrm(ks[5], (DEPTH, 6 * D), 0.01),
        'g_mix': 1.0 + nrm(ks[6], (DEPTH, D), 0.02),
        'g_ffn': 1.0 + nrm(ks[7], (DEPTH, D), 0.02),
        'w_in_even': nrm(ks[8], (N_EVEN, D, EVEN_IN), D ** -0.5),
        'w_out_even': nrm(ks[9], (N_EVEN, EVEN_MIX, D), EVEN_MIX ** -0.5),
        'a_q_gain': 1.0 + nrm(ks[10], (N_EVEN, HEAD_DIM), 0.02),
        'a_k_gain': 1.0 + nrm(ks[11], (N_EVEN, HEAD_DIM), 0.02),
        'a_lambda': nrm(ks[12], (N_EVEN, 4, HEAD_DIM), 0.1),
        'a_subln': 1.0 + nrm(ks[13], (N_EVEN, A_VDIM), 0.02),
        'b_w_af': nrm(ks[14], (N_EVEN, GATE_RANK, B_HEADS * B_DK), GATE_RANK ** -0.5),
        'b_b_af': 1.0 + nrm(ks[15], (N_EVEN, B_HEADS * B_DK), 0.5),
        'b_w_ab': nrm(ks[16], (N_EVEN, GATE_RANK, B_HEADS * B_DK), GATE_RANK ** -0.5),
        'b_b_ab': 1.0 + nrm(ks[17], (N_EVEN, B_HEADS * B_DK), 0.5),
        'b_gain': 1.0 + nrm(ks[18], (N_EVEN, B_DV), 0.02),
        'w_in_odd': nrm(ks[19], (N_ODD, D, ODD_IN), D ** -0.5),
        'w_out_odd': nrm(ks[20], (N_ODD, C_WIDTH, D), C_WIDTH ** -0.5),
        'c_q_gain': 1.0 + nrm(ks[21], (N_ODD, HEAD_DIM), 0.02),
        'c_k_gain': 1.0 + nrm(ks[22], (N_ODD, HEAD_DIM), 0.02),
        'peer_wq': nrm(ks[23], (DEPTH, D, PEER_HEADS * PEER_DK), D ** -0.5),
        'peer_keys': nrm(ks[24], (DEPTH, PEER_HEADS, 2, PEER_NKEYS, PEER_DK // 2), (PEER_DK // 2) ** -0.5),
        'peer_u': nrm(ks[25], (DEPTH, PEER_EXPERTS, D), D ** -0.5),
        'peer_v': nrm(ks[26], (DEPTH, PEER_EXPERTS, D), PEER_V_SCALE),
    }


def reference(x, c, ctx, c_ctx, w_mod, b_mod, g_mix, g_ffn,
              w_in_even, w_out_even, a_q_gain, a_k_gain, a_lambda, a_subln,
              b_w_af, b_b_af, b_w_ab, b_b_ab, b_gain,
              w_in_odd, w_out_odd, c_q_gain, c_k_gain,
              peer_wq, peer_keys, peer_u, peer_v):
    cos, sin = axial_rope_tables(x.shape[1])
    h_c = ctx
    silu_c = jax.nn.silu(c)
    silu_cc = jax.nn.silu(c_ctx)[None, :]
    for layer in range(DEPTH):
        need_ctx = layer < DEPTH - 1
        mx = jnp.split((silu_c @ w_mod[layer] + b_mod[layer])[:, None, :], 6, axis=-1)
        mc = jnp.split((silu_cc @ w_mod[layer] + b_mod[layer])[:, None, :], 6, axis=-1)
        a_x = modulate(rms_norm(x, g_mix[layer]), mx[0], mx[1])
        a_c = modulate(rms_norm(h_c, g_mix[layer]), mc[0], mc[1])
        i = layer // 2
        if layer % 2 == 0:
            lam_init = 0.8 - 0.6 * math.exp(-0.3 * layer)
            y_x, y_c = even_mixer(a_x, a_c, cos, sin, w_in_even[i], w_out_even[i],
                                  a_q_gain[i], a_k_gain[i], a_lambda[i], a_subln[i],
                                  b_w_af[i], b_b_af[i], b_w_ab[i], b_b_ab[i], b_gain[i],
                                  lam_init, need_ctx)
        else:
            y_x, y_c = odd_mixer(a_x, a_c, cos, sin, w_in_odd[i], w_out_odd[i],
                                 c_q_gain[i], c_k_gain[i], need_ctx)
        x = x + mx[2] * y_x
        f_x = peer_ffn(modulate(rms_norm(x, g_ffn[layer]), mx[3], mx[4]),
                       peer_wq[layer], peer_keys[layer], peer_u[layer], peer_v[layer])
        x = x + mx[5] * f_x
        if need_ctx:
            h_c = h_c + mc[2] * y_c
            f_c = peer_ffn(modulate(rms_norm(h_c, g_ffn[layer]), mc[3], mc[4]),
                           peer_wq[layer], peer_keys[layer], peer_u[layer], peer_v[layer])
            h_c = h_c + mc[5] * f_c
    return x
```

```python
import functools
import math

import numpy as np
import jax
import jax.numpy as jnp
from jax import lax
from jax.experimental import pallas as pl
from jax.experimental.pallas import tpu as pltpu

F32 = jnp.float32
BF16 = jnp.bfloat16

EPS = 1e-6
GRID_W = 64
HEAD_DIM = 64
ROPE_FREQS = HEAD_DIM // 4
ROPE_THETA = 10000.0
A_HEADS = 4
B_HEADS = 4
B_DK = 64
B_DV = 128
GATE_RANK = 16
GATE_TAU = 16.0
GLA_CHUNK = 64
C_Q_HEADS = 16
C_KV_HEADS = 4
PEER_HEADS = 8
PEER_NKEYS = 128
PEER_DK = 128
PEER_TOPK = 16

LANES = 128
SUBLANES = 8
VMEM_LIMIT = 56 * 1024 * 1024
NEG_BIG = -3.0e38

TOK_TILE = 256
PEER_TOK_TILE = 768
PEER_EXPERT_BLOCK = 512


def _dot(a, b):
    return jnp.dot(a, b, preferred_element_type=F32)


def _dot_nt(a, b):
    return lax.dot_general(a, b, (((1,), (1,)), ((), ())), preferred_element_type=F32)


def _split_dot(onehot_bf16, x, parts):
    acc = None
    rem = x
    for _ in range(parts):
        piece = rem.astype(BF16)
        rem = rem - piece.astype(F32)
        t = _dot(onehot_bf16, piece)
        acc = t if acc is None else acc + t
    return acc


def _split_dot_right(x, onehot_bf16, parts):
    acc = None
    rem = x
    for _ in range(parts):
        piece = rem.astype(BF16)
        rem = rem - piece.astype(F32)
        t = _dot(piece, onehot_bf16)
        acc = t if acc is None else acc + t
    return acc


def _rms_mod(x, g, shift, scale):
    ms = jnp.mean(x * x, axis=-1, keepdims=True)
    y = x * lax.rsqrt(ms + EPS) * g
    return y * (1.0 + scale) + shift


def _pick_mod(mx_ref, mc_ref, k, is_ctx):
    return jnp.where(is_ctx, mc_ref[k:k + 1, :], mx_ref[0, k:k + 1, :])


def _mod_kernel(c_ref, w_ref, b_ref, o_ref):
    c = c_ref[...]
    s = c / (1.0 + jnp.exp(-c))
    o_ref[0] = _dot(s, w_ref[0]) + b_ref[0]


def _modulation(cc, w_mod, b_mod):
    depth, d, n = w_mod.shape
    rows = cc.shape[0]
    tn = d
    return pl.pallas_call(
        _mod_kernel,
        grid=(depth, n // tn),
        in_specs=[
            pl.BlockSpec((rows, d), lambda l, j: (0, 0)),
            pl.BlockSpec((1, d, tn), lambda l, j: (l, 0, j)),
            pl.BlockSpec((1, 1, tn), lambda l, j: (l, 0, j)),
        ],
        out_specs=pl.BlockSpec((1, rows, tn), lambda l, j: (l, 0, j)),
        out_shape=jax.ShapeDtypeStruct((depth, rows, n), F32),
        compiler_params=pltpu.CompilerParams(
            dimension_semantics=("arbitrary", "arbitrary"), vmem_limit_bytes=VMEM_LIMIT),
    )(cc, w_mod, b_mod.reshape(depth, 1, n))


def _head_norm_rope(t, seg, gain, cos, sin_signed, first_half):
    outs = []
    for c0 in range(0, t.shape[1], 512):
        tc = t[:, c0:c0 + 512]
        ss = _split_dot_right(tc * tc, seg, 2)
        tn = tc * lax.rsqrt(ss * (1.0 / HEAD_DIM) + EPS) * gain[:, c0:c0 + 512]
        for c in range(0, 512, LANES):
            b = tn[:, c:c + LANES]
            fwd = pltpu.roll(b, LANES - ROPE_FREQS, axis=1)
            bwd = pltpu.roll(b, ROPE_FREQS, axis=1)
            partner = jnp.where(first_half, fwd, bwd)
            outs.append(b * cos + partner * sin_signed)
    return jnp.concatenate(outs, axis=1)


def _inproj_kernel(*refs, n_ctx_tiles, n_norm, n_lowp, gla):
    if gla:
        (x_ref, mx_ref, mc_ref, g_ref, w_ref, gain_ref, cos_ref, sin_ref, seg_ref,
         wg_ref, waf_ref, wab_ref, bf_ref, bb_ref, qkv_ref, gl_ref) = refs
    else:
        (x_ref, mx_ref, mc_ref, g_ref, w_ref, gain_ref, cos_ref, sin_ref, seg_ref,
         qkv_ref) = refs
    is_ctx = pl.program_id(1) < n_ctx_tiles
    shift = _pick_mod(mx_ref, mc_ref, 0, is_ctx)
    scale = _pick_mod(mx_ref, mc_ref, 1, is_ctx)
    a = _rms_mod(x_ref[0], g_ref[...], shift, scale).astype(BF16)
    p = _dot(a, w_ref[...])
    lane = lax.broadcasted_iota(jnp.int32, (1, LANES), 1)
    first_half = (lane % (2 * ROPE_FREQS)) < ROPE_FREQS
    qk = _head_norm_rope(p[:, :n_norm], seg_ref[...], gain_ref[...], cos_ref[...], sin_ref[...],
                         first_half)
    qkv_ref[0, :, :n_norm] = qk.astype(BF16)
    qkv_ref[0, :, n_norm:] = p[:, n_norm:n_lowp].astype(BF16)
    if gla:
        nq = B_HEADS * B_DK
        rest = p[:, n_lowp:]
        gl_ref[0, :, :nq] = rest[:, :nq] * (B_DK ** -0.5)
        gl_ref[0, :, nq:rest.shape[1]] = rest[:, nq:]
        pg = _dot(a, wg_ref[...]).astype(BF16)
        for wref, bref, off in ((waf_ref, bf_ref, 0), (wab_ref, bb_ref, nq)):
            z = _dot(pg, wref[...]) + bref[...]
            la = (jnp.minimum(z, 0.0) - jnp.log(1.0 + jnp.exp(-jnp.abs(z)))) * (1.0 / GATE_TAU)
            gl_ref[0, :, rest.shape[1] + off:rest.shape[1] + off + nq] = la


def _inproj(xs, mx, mc, g, w, gain, cos_t, sin_t, seg, n_ctx_tiles, n_norm, gla_ops=None):
    b, l, d = xs.shape
    tm = TOK_TILE
    n = w.shape[1]
    gla = gla_ops is not None
    n_lowp = n - (2 * B_HEADS * B_DK + 2 * B_HEADS * B_DV if gla else 0)
    full = lambda shape: pl.BlockSpec(shape, lambda i, j: (0,) * len(shape))
    in_specs = [
        pl.BlockSpec((1, tm, d), lambda i, j: (i, j, 0)),
        pl.BlockSpec((1, 8, d), lambda i, j: (i, 0, 0)),
        full((8, d)), full((1, d)), full((d, n)), full((1, n_norm)),
        pl.BlockSpec((tm, LANES), lambda i, j: (j, 0)),
        pl.BlockSpec((tm, LANES), lambda i, j: (j, 0)),
        full((512, 512)),
    ]
    ops = [xs, mx, mc, g, w, gain, cos_t, sin_t, seg]
    out_shape = [jax.ShapeDtypeStruct((b, l, n_lowp), BF16)]
    out_specs = [pl.BlockSpec((1, tm, n_lowp), lambda i, j: (i, j, 0))]
    if gla:
        wg, waf, wab, bf, bb = gla_ops
        in_specs += [full(wg.shape), full(waf.shape), full(wab.shape), full(bf.shape), full(bb.shape)]
        ops += [wg, waf, wab, bf, bb]
        n_gl = (n - n_lowp) + 2 * B_HEADS * B_DK
        out_shape.append(jax.ShapeDtypeStruct((b, l, n_gl), F32))
        out_specs.append(pl.BlockSpec((1, tm, n_gl), lambda i, j: (i, j, 0)))
    return pl.pallas_call(
        functools.partial(_inproj_kernel, n_ctx_tiles=n_ctx_tiles, n_norm=n_norm, n_lowp=n_lowp, gla=gla),
        grid=(b, l // tm),
        in_specs=in_specs,
        out_specs=out_specs,
        out_shape=out_shape,
        compiler_params=pltpu.CompilerParams(
            dimension_semantics=("parallel", "arbitrary"), vmem_limit_bytes=VMEM_LIMIT),
    )(*ops)


def _softmax_parts(s):
    m = jnp.max(s, axis=-1, keepdims=True)
    e = jnp.exp(s - m)
    return e, 1.0 / jnp.sum(e, axis=-1, keepdims=True)


def _diff_body(q, k, v, lam, subln, post_scale):
    lane = lax.broadcasted_iota(jnp.int32, q.shape, 1)
    lo = lane < HEAD_DIM
    zero = jnp.zeros_like(q)
    e0, r0 = _softmax_parts(_dot_nt(jnp.where(lo, q, zero), k))
    e1, r1 = _softmax_parts(_dot_nt(jnp.where(lo, zero, q), k))
    w = e0 * r0 - e1 * (r1 * lam)
    o = _dot(w.astype(BF16), v)
    ms = jnp.mean(o * o, axis=-1, keepdims=True)
    return o * lax.rsqrt(ms + EPS) * subln * post_scale


def _diff_attn_kernel(q_ref, k_ref, v_ref, lam_ref, subln_ref, o_ref, *, n_ctx_tiles, ctx_len, lam_init):
    lv = lam_ref[...]
    lam = (jnp.exp(jnp.sum(lv[0:1] * lv[1:2], axis=-1, keepdims=True))
           - jnp.exp(jnp.sum(lv[2:3] * lv[3:4], axis=-1, keepdims=True)) + lam_init)
    is_ctx = pl.program_id(2) < n_ctx_tiles

    @pl.when(is_ctx)
    def _():
        o_ref[0] = _diff_body(q_ref[0], k_ref[0, :ctx_len], v_ref[0, :ctx_len], lam, subln_ref[...],
                              1.0 - lam_init).astype(o_ref.dtype)

    @pl.when(jnp.logical_not(is_ctx))
    def _():
        o_ref[0] = _diff_body(q_ref[0], k_ref[0], v_ref[0], lam, subln_ref[...],
                              1.0 - lam_init).astype(o_ref.dtype)


def _diff_attn(qkv, lam_vec, subln, n_ctx_tiles, ctx_len, lam_init):
    b, l, _ = qkv.shape
    tq = TOK_TILE
    w = 2 * HEAD_DIM
    return pl.pallas_call(
        functools.partial(_diff_attn_kernel, n_ctx_tiles=n_ctx_tiles, ctx_len=ctx_len, lam_init=lam_init),
        grid=(b, A_HEADS, l // tq),
        in_specs=[
            pl.BlockSpec((1, tq, w), lambda i, h, j: (i, j, h)),
            pl.BlockSpec((1, l, w), lambda i, h, j: (i, 0, A_HEADS + h)),
            pl.BlockSpec((1, l, w), lambda i, h, j: (i, 0, 2 * A_HEADS + h)),
            pl.BlockSpec((4, HEAD_DIM), lambda i, h, j: (0, 0)),
            pl.BlockSpec((1, w), lambda i, h, j: (0, 0)),
        ],
        out_specs=pl.BlockSpec((1, tq, w), lambda i, h, j: (i, j, h)),
        out_shape=jax.ShapeDtypeStruct((b, l, A_HEADS * w), BF16),
        compiler_params=pltpu.CompilerParams(
            dimension_semantics=("parallel", "parallel", "arbitrary"), vmem_limit_bytes=VMEM_LIMIT),
    )(qkv, qkv, qkv, lam_vec, subln)


def _gqa_body(q, k, v, o_ref):
    group = C_Q_HEADS // C_KV_HEADS
    for kv in range(C_KV_HEADS):
        kd = k[:, kv * LANES:(kv + 1) * LANES]
        vd = v[:, kv * LANES:(kv + 1) * LANES]
        for a in range(group // 2):
            c0 = (kv * group + 2 * a) * HEAD_DIM
            qp = q[:, c0:c0 + LANES]
            lane = lax.broadcasted_iota(jnp.int32, qp.shape, 1)
            lo = lane < HEAD_DIM
            zero = jnp.zeros_like(qp)
            e0, r0 = _softmax_parts(_dot_nt(jnp.where(lo, qp, zero), kd))
            e1, r1 = _softmax_parts(_dot_nt(jnp.where(lo, zero, qp), kd))
            o0 = _dot((e0 * r0).astype(BF16), vd)
            o1 = _dot((e1 * r1).astype(BF16), vd)
            o_ref[0, :, c0:c0 + LANES] = jnp.where(lo, o0, o1).astype(o_ref.dtype)


def _gqa_attn_kernel(q_ref, k_ref, v_ref, o_ref, *, n_ctx_tiles, ctx_len):
    is_ctx = pl.program_id(1) < n_ctx_tiles

    @pl.when(is_ctx)
    def _():
        _gqa_body(q_ref[0], k_ref[0, :ctx_len], v_ref[0, :ctx_len], o_ref)

    @pl.when(jnp.logical_not(is_ctx))
    def _():
        _gqa_body(q_ref[0], k_ref[0], v_ref[0], o_ref)


def _gqa_attn(qkv, n_ctx_tiles, ctx_len):
    b, l, _ = qkv.shape
    tq = TOK_TILE
    wq = C_Q_HEADS * HEAD_DIM
    wk = 2 * C_KV_HEADS * HEAD_DIM
    return pl.pallas_call(
        functools.partial(_gqa_attn_kernel, n_ctx_tiles=n_ctx_tiles, ctx_len=ctx_len),
        grid=(b, l // tq),
        in_specs=[
            pl.BlockSpec((1, tq, wq), lambda i, j: (i, j, 0)),
            pl.BlockSpec((1, l, wk), lambda i, j: (i, 0, wq // wk)),
            pl.BlockSpec((1, l, wk), lambda i, j: (i, 0, wq // wk + 1)),
        ],
        out_specs=pl.BlockSpec((1, tq, wq), lambda i, j: (i, j, 0)),
        out_shape=jax.ShapeDtypeStruct((b, l, wq), BF16),
        compiler_params=pltpu.CompilerParams(
            dimension_semantics=("parallel", "arbitrary"), vmem_limit_bytes=VMEM_LIMIT),
    )(qkv, qkv, qkv)


def _gla_tile(q, k, v, la, st_ref, reverse):
    t = q.shape[0]
    row = lax.broadcasted_iota(jnp.int32, (t, t), 0)
    col = lax.broadcasted_iota(jnp.int32, (t, t), 1)
    same = jnp.where((row // GLA_CHUNK) == (col // GLA_CHUNK), 1.0, 0.0)
    tri = jnp.where((col >= row) if reverse else (col <= row), same, 0.0)
    cum = _split_dot(tri.astype(BF16), la, 3)
    tot = _split_dot(same.astype(BF16), la, 3)
    q_dec = q * jnp.exp(cum)
    k_inv = (k * jnp.exp(-cum)).astype(BF16)
    k_end = (k * jnp.exp(tot - cum)).astype(BF16)
    lane = lax.broadcasted_iota(jnp.int32, q.shape, 1)
    lo = lane < B_DK
    qz = jnp.zeros_like(q_dec)
    vb = v.astype(BF16)
    keep = tri > 0.5
    a_lo = jnp.where(keep, _dot_nt(jnp.where(lo, q_dec, qz).astype(BF16), k_inv), 0.0).astype(BF16)
    a_hi = jnp.where(keep, _dot_nt(jnp.where(lo, qz, q_dec).astype(BF16), k_inv), 0.0).astype(BF16)
    o_intra = jnp.concatenate([_dot(a_lo, vb[:, :B_DV]), _dot(a_hi, vb[:, B_DV:])], axis=1)
    qd = q_dec.astype(BF16)
    srow = lax.broadcasted_iota(jnp.int32, (2 * B_DV, 2 * B_DK), 0)
    scol = lax.broadcasted_iota(jnp.int32, (2 * B_DV, 2 * B_DK), 1)
    own = (srow < B_DV) == (scol < B_DK)
    n_chunks = t // GLA_CHUNK
    outs = [None] * n_chunks
    for c in (range(n_chunks - 1, -1, -1) if reverse else range(n_chunks)):
        r0 = c * GLA_CHUNK
        st = st_ref[...]
        outs[c] = o_intra[r0:r0 + GLA_CHUNK] + _dot_nt(qd[r0:r0 + GLA_CHUNK], st.astype(BF16))
        decay = jnp.exp(tot[r0:r0 + 1, :])
        d_state = _dot(v[r0:r0 + GLA_CHUNK].T.astype(BF16), k_end[r0:r0 + GLA_CHUNK])
        st_ref[...] = st * decay + jnp.where(own, d_state, 0.0)
    return jnp.concatenate(outs, axis=0)


def _gla_kernel(qf, kf, vf, lf, qb, kb, vb, lb, of_ref, ob_ref, stf_ref, stb_ref):
    @pl.when(pl.program_id(2) == 0)
    def _():
        stf_ref[...] = jnp.zeros_like(stf_ref)
        stb_ref[...] = jnp.zeros_like(stb_ref)

    of_ref[0] = _gla_tile(qf[0], kf[0], vf[0], lf[0], stf_ref, False)
    ob_ref[0] = _gla_tile(qb[0], kb[0], vb[0], lb[0], stb_ref, True)


def _gla(gl, n_ctx_tiles):
    b, l, _ = gl.shape
    t = TOK_TILE
    nt = l // t
    pairs = B_HEADS // 2
    wk = 2 * B_DK
    wv = 2 * B_DV
    nq = B_HEADS * B_DK
    la_f0 = (2 * nq + 2 * B_HEADS * B_DV) // wk

    def bwd(j):
        return jnp.where(j < n_ctx_tiles, n_ctx_tiles - 1 - j, nt - 1 - (j - n_ctx_tiles))

    def specs(tile):
        return [
            pl.BlockSpec((1, t, wk), lambda i, p, j: (i, tile(j), p)),
            pl.BlockSpec((1, t, wk), lambda i, p, j: (i, tile(j), nq // wk + p)),
            pl.BlockSpec((1, t, wv), lambda i, p, j: (i, tile(j), 2 * nq // wv + p)),
        ]

    fwd = lambda j: j
    in_specs = (specs(fwd) + [pl.BlockSpec((1, t, wk), lambda i, p, j: (i, j, la_f0 + p))]
                + specs(bwd) + [pl.BlockSpec((1, t, wk), lambda i, p, j: (i, bwd(j), la_f0 + nq // wk + p))])
    return pl.pallas_call(
        _gla_kernel,
        grid=(b, pairs, nt),
        in_specs=in_specs,
        out_specs=[pl.BlockSpec((1, t, wv), lambda i, p, j: (i, j, p)),
                   pl.BlockSpec((1, t, wv), lambda i, p, j: (i, bwd(j), p))],
        out_shape=[jax.ShapeDtypeStruct((b, l, B_HEADS * B_DV), F32)] * 2,
        scratch_shapes=[pltpu.VMEM((wv, wk), F32), pltpu.VMEM((wv, wk), F32)],
        compiler_params=pltpu.CompilerParams(
            dimension_semantics=("parallel", "parallel", "arbitrary"), vmem_limit_bytes=VMEM_LIMIT),
    )(*([gl] * 8))


def _outproj_even_kernel(x_ref, mx_ref, mc_ref, d_ref, of_ref, ob_ref, r_ref, gain_ref, w_ref, o_ref,
                         *, n_ctx_tiles):
    is_ctx = pl.program_id(1) < n_ctx_tiles
    gate = _pick_mod(mx_ref, mc_ref, 2, is_ctx)
    o = of_ref[0] + ob_ref[0]
    r = r_ref[0]
    parts = []
    for c in range(0, o.shape[1], B_DV):
        oc = o[:, c:c + B_DV]
        ms = jnp.mean(oc * oc, axis=-1, keepdims=True)
        parts.append(oc * lax.rsqrt(ms + EPS))
    g = jnp.concatenate(parts, axis=1) * gain_ref[...] * (r / (1.0 + jnp.exp(-r)))
    nd = d_ref.shape[2]
    y = _dot(d_ref[0], w_ref[:nd, :]) + _dot(g.astype(BF16), w_ref[nd:, :])
    o_ref[0] = x_ref[0] + gate * y


def _outproj_odd_kernel(x_ref, mx_ref, mc_ref, a_ref, w_ref, o_ref, *, n_ctx_tiles):
    is_ctx = pl.program_id(1) < n_ctx_tiles
    gate = _pick_mod(mx_ref, mc_ref, 2, is_ctx)
    o_ref[0] = x_ref[0] + gate * _dot(a_ref[0], w_ref[...])


def _outproj(xs, mx, mc, w, n_ctx_tiles, attn, gla_ops=None):
    b, l, d = xs.shape
    tm = TOK_TILE
    tok = lambda width, blk=0: pl.BlockSpec((1, tm, width), lambda i, j: (i, j, blk))
    full = lambda shape: pl.BlockSpec(shape, lambda i, j: (0,) * len(shape))
    head = [tok(d), pl.BlockSpec((1, 8, d), lambda i, j: (i, 0, 0)), full((8, d))]
    if gla_ops is not None:
        o_f, o_b, gl, gain = gla_ops
        wv = B_HEADS * B_DV
        r_blk = (2 * B_HEADS * B_DK + wv) // wv
        kern = functools.partial(_outproj_even_kernel, n_ctx_tiles=n_ctx_tiles)
        in_specs = head + [tok(attn.shape[2]), tok(wv), tok(wv), tok(wv, r_blk), full((1, wv)), full(w.shape)]
        ops = [xs, mx, mc, attn, o_f, o_b, gl, gain, w]
    else:
        kern = functools.partial(_outproj_odd_kernel, n_ctx_tiles=n_ctx_tiles)
        in_specs = head + [tok(attn.shape[2]), full(w.shape)]
        ops = [xs, mx, mc, attn, w]
    return pl.pallas_call(
        kern,
        grid=(b, l // tm),
        in_specs=in_specs,
        out_specs=tok(d),
        out_shape=jax.ShapeDtypeStruct((b, l, d), F32),
        compiler_params=pltpu.CompilerParams(
            dimension_semantics=("parallel", "arbitrary"), vmem_limit_bytes=VMEM_LIMIT),
    )(*ops)


def _ce(v, i, j):
    a, b = v[i], v[j]
    v[i] = jnp.maximum(a, b)
    v[j] = jnp.minimum(a, b)


def _bitonic_sort_desc(v):
    n = len(v)
    k = 2
    while k <= n:
        j = k // 2
        while j >= 1:
            for i in range(n):
                m = i ^ j
                if m > i:
                    if (i & k) == 0:
                        _ce(v, i, m)
                    else:
                        _ce(v, m, i)
            j //= 2
        k *= 2


def _bitonic_merge_desc(v):
    n = len(v)
    j = n // 2
    while j >= 1:
        for i in range(n):
            m = i ^ j
            if m > i:
                _ce(v, i, m)
        j //= 2


def _merge_top(a, b):
    n = len(a)
    c = [jnp.maximum(a[i], b[n - 1 - i]) for i in range(n)]
    _bitonic_merge_desc(c)
    return c


def _top_sorted(s):
    groups = s.shape[0] // SUBLANES
    v = [s[g * SUBLANES:(g + 1) * SUBLANES, :] for g in range(groups)]
    _bitonic_sort_desc(v)
    v = v[:PEER_TOPK]
    shift = SUBLANES // 2
    while shift >= 1:
        v = _merge_top(v, [pltpu.roll(x, shift, axis=0) for x in v])
        shift //= 2
    return v


def _gelu_tanh(x):
    return 0.5 * x * (1.0 + jnp.tanh(0.7978845608028654 * (x + 0.044715 * (x * x * x))))


def _peer_kernel(x_ref, mx_ref, mc_ref, g_ref, wqt_ref, keys_ref, u_ref, vt_ref, o_ref,
                 ht_ref, acc_ref, s1_ref, s2_ref, e1_ref, e2_ref, thr_ref, *, ctx_len, rows_i):
    tm = x_ref.shape[1]
    eb = pl.program_id(2)
    n_eb = pl.num_programs(2)
    row = pl.program_id(1) * tm + lax.broadcasted_iota(jnp.int32, (tm, 1), 0)
    is_ctx = row < ctx_len
    half = PEER_DK // 2

    @pl.when(eb == 0)
    def _():
        shift = jnp.where(is_ctx, mc_ref[3:4, :], mx_ref[0, 3:4, :])
        scale = jnp.where(is_ctx, mc_ref[4:5, :], mx_ref[0, 4:5, :])
        h = _rms_mod(x_ref[0], g_ref[...], shift, scale)
        ht = h.T.astype(BF16)
        ht_ref[...] = ht
        acc_ref[...] = jnp.zeros_like(acc_ref)
        qt = _dot(wqt_ref[...], ht).astype(BF16)
        sub = lax.broadcasted_iota(jnp.int32, (SUBLANES, tm), 0)
        a_all = [jnp.zeros((SUBLANES, tm), F32)] * PEER_TOPK
        b_all = [jnp.zeros((SUBLANES, tm), F32)] * PEER_TOPK
        for hd in range(PEER_HEADS):
            s1 = _dot(keys_ref[hd, 0], qt[hd * PEER_DK:hd * PEER_DK + half, :])
            s2 = _dot(keys_ref[hd, 1], qt[hd * PEER_DK + half:(hd + 1) * PEER_DK, :])
            s1_ref[hd] = s1
            s2_ref[hd] = s2
            ta = _top_sorted(s1)
            tb = _top_sorted(s2)
            a_all = [jnp.where(sub == hd, ta[k], a_all[k]) for k in range(PEER_TOPK)]
            b_all = [jnp.where(sub == hd, tb[k], b_all[k]) for k in range(PEER_TOPK)]
        neg = jnp.full((SUBLANES, tm), NEG_BIG, F32)
        top = [a_all[0] + b_all[c] for c in range(PEER_TOPK)]
        for r in range(1, PEER_TOPK):
            n_r = PEER_TOPK // (r + 1)
            top = _merge_top(top, [a_all[r] + b_all[c] for c in range(n_r)] + [neg] * (PEER_TOPK - n_r))
        z = jnp.zeros((SUBLANES, tm), F32)
        for k in range(PEER_TOPK):
            z = z + jnp.exp(top[k] - top[0])
        thr_ref[...] = top[PEER_TOPK - 1]
        inv_z = 1.0 / z
        for hd in range(PEER_HEADS):
            e1_ref[hd] = jnp.exp(s1_ref[hd] - a_all[0][hd:hd + 1, :]) * inv_z[hd:hd + 1, :]
            e2_ref[hd] = jnp.exp(s2_ref[hd] - b_all[0][hd:hd + 1, :])

    at = _dot(u_ref[...], ht_ref[...])
    wt = []
    for ii in range(rows_i):
        i = eb * rows_i + ii
        g = jnp.zeros((PEER_NKEYS, tm), F32)
        for hd in range(PEER_HEADS):
            s1row = s1_ref[hd, pl.ds(i, 1), :]
            e1row = e1_ref[hd, pl.ds(i, 1), :]
            sel = (s1row + s2_ref[hd]) >= thr_ref[hd:hd + 1, :]
            g = g + jnp.where(sel, e1row * e2_ref[hd], 0.0)
        wt.append((_gelu_tanh(at[ii * PEER_NKEYS:(ii + 1) * PEER_NKEYS, :]) * g).astype(BF16))
    acc_ref[...] += _dot(vt_ref[...], jnp.concatenate(wt, axis=0))

    @pl.when(eb == n_eb - 1)
    def _():
        gate = jnp.where(is_ctx, mc_ref[5:6, :], mx_ref[0, 5:6, :])
        o_ref[0] = x_ref[0] + gate * acc_ref[...].T


def _peer(xs, mx, mc, g, wqt, keys, u, vt, ctx_len):
    b, l, d = xs.shape
    tm = PEER_TOK_TILE
    eblk = PEER_EXPERT_BLOCK
    n_exp = u.shape[0]
    full = lambda shape: pl.BlockSpec(shape, lambda i, j, e: (0,) * len(shape))
    sc = (PEER_HEADS, PEER_NKEYS, tm)
    return pl.pallas_call(
        functools.partial(_peer_kernel, ctx_len=ctx_len, rows_i=eblk // PEER_NKEYS),
        grid=(b, l // tm, n_exp // eblk),
        in_specs=[
            pl.BlockSpec((1, tm, d), lambda i, j, e: (i, j, 0)),
            pl.BlockSpec((1, 8, d), lambda i, j, e: (i, 0, 0)),
            full((8, d)), full((1, d)), full(wqt.shape), full(keys.shape),
            pl.BlockSpec((eblk, d), lambda i, j, e: (e, 0)),
            pl.BlockSpec((d, eblk), lambda i, j, e: (0, e)),
        ],
        out_specs=pl.BlockSpec((1, tm, d), lambda i, j, e: (i, j, 0)),
        out_shape=jax.ShapeDtypeStruct((b, l, d), F32),
        scratch_shapes=[
            pltpu.VMEM((d, tm), BF16), pltpu.VMEM((d, tm), F32),
            pltpu.VMEM(sc, F32), pltpu.VMEM(sc, F32), pltpu.VMEM(sc, F32), pltpu.VMEM(sc, F32),
            pltpu.VMEM((PEER_HEADS, tm), F32),
        ],
        compiler_params=pltpu.CompilerParams(
            dimension_semantics=("parallel", "parallel", "arbitrary"), vmem_limit_bytes=VMEM_LIMIT),
    )(xs, mx, mc, g, wqt, keys, u, vt)


def kernel(x, c, ctx, c_ctx, w_mod, b_mod, g_mix, g_ffn, w_in_even, w_out_even, a_q_gain, a_k_gain,
           a_lambda, a_subln, b_w_af, b_b_af, b_w_ab, b_b_ab, b_gain, w_in_odd, w_out_odd, c_q_gain,
           c_k_gain, peer_wq, peer_keys, peer_u, peer_v):
    bsz, seq, d = x.shape
    n_ctx = ctx.shape[1]
    depth = w_mod.shape[0]
    n_ctx_tiles = n_ctx // TOK_TILE
    assert n_ctx % TOK_TILE == 0 and seq % TOK_TILE == 0 and (n_ctx + seq) % PEER_TOK_TILE == 0

    rows = seq // GRID_W
    row = jnp.repeat(jnp.arange(rows, dtype=F32), GRID_W)
    col = jnp.tile(jnp.arange(GRID_W, dtype=F32), rows)
    inv_freq = ROPE_THETA ** (-jnp.arange(ROPE_FREQS, dtype=F32) / ROPE_FREQS)
    ang = jnp.stack([row, col], axis=-1)[:, :, None] * inv_freq
    cos = jnp.cos(ang)
    sin = jnp.sin(ang)
    cos64 = jnp.concatenate([cos[:, 0], cos[:, 0], cos[:, 1], cos[:, 1]], axis=-1)
    sin64 = jnp.concatenate([-sin[:, 0], sin[:, 0], -sin[:, 1], sin[:, 1]], axis=-1)
    cos_t = jnp.concatenate([jnp.ones((n_ctx, LANES), F32), jnp.tile(cos64, (1, 2))], axis=0)
    sin_t = jnp.concatenate([jnp.zeros((n_ctx, LANES), F32), jnp.tile(sin64, (1, 2))], axis=0)
    seg = jnp.asarray(np.kron(np.eye(512 // HEAD_DIM), np.ones((HEAD_DIM, HEAD_DIM))), BF16)

    pad = (-(bsz + 1)) % SUBLANES
    cc = jnp.concatenate([c, c_ctx[None, :], jnp.zeros((pad, d), F32)], axis=0)
    mod = _modulation(cc, w_mod, b_mod)
    mod = mod.reshape(depth, bsz + 1 + pad, 6, d)
    mod = jnp.concatenate([mod, jnp.zeros((depth, bsz + 1 + pad, 2, d), F32)], axis=2)

    xs = jnp.concatenate([ctx, x], axis=1)
    scale = HEAD_DIM ** -0.5
    for layer in range(depth):
        mx = mod[layer, :bsz]
        mc = mod[layer, bsz]
        i = layer // 2
        g1 = g_mix[layer][None, :]
        if layer % 2 == 0:
            lam_init = 0.8 - 0.6 * math.exp(-0.3 * layer)
            w = w_in_even[i]
            n_main = w.shape[1] - 2 * GATE_RANK
            gain = jnp.concatenate([jnp.tile(a_q_gain[i] * scale, 2 * A_HEADS),
                                    jnp.tile(a_k_gain[i], 2 * A_HEADS)])[None, :]
            wg = jnp.pad(w[:, n_main:], ((0, 0), (0, LANES - 2 * GATE_RANK))).astype(BF16)
            waf = jnp.pad(b_w_af[i], ((0, LANES - GATE_RANK), (0, 0))).astype(BF16)
            wab = jnp.pad(b_w_ab[i], ((GATE_RANK, LANES - 2 * GATE_RANK), (0, 0))).astype(BF16)
            qkv, gl = _inproj(xs, mx, mc, g1, w[:, :n_main].astype(BF16), gain, cos_t, sin_t, seg,
                              n_ctx_tiles, gain.shape[1],
                              gla_ops=(wg, waf, wab, b_b_af[i][None, :], b_b_ab[i][None, :]))
            dx = _diff_attn(qkv, a_lambda[i], a_subln[i][None, :], n_ctx_tiles, n_ctx, lam_init)
            o_f, o_b = _gla(gl, n_ctx_tiles)
            xs = _outproj(xs, mx, mc, w_out_even[i].astype(BF16), n_ctx_tiles, dx,
                          gla_ops=(o_f, o_b, gl, jnp.tile(b_gain[i], B_HEADS)[None, :]))
        else:
            w = w_in_odd[i]
            nq = C_Q_HEADS * HEAD_DIM
            nk = C_KV_HEADS * HEAD_DIM
            dup = lambda m: jnp.repeat(m.reshape(d, C_KV_HEADS, 1, HEAD_DIM), 2, axis=2).reshape(d, 2 * nk)
            w2 = jnp.concatenate([w[:, :nq], dup(w[:, nq:nq + nk]), dup(w[:, nq + nk:])], axis=1)
            gain = jnp.concatenate([jnp.tile(c_q_gain[i] * scale, C_Q_HEADS),
                                    jnp.tile(c_k_gain[i], 2 * C_KV_HEADS)])[None, :]
            qkv = _inproj(xs, mx, mc, g1, w2.astype(BF16), gain, cos_t, sin_t, seg,
                          n_ctx_tiles, gain.shape[1])[0]
            att = _gqa_attn(qkv, n_ctx_tiles, n_ctx)
            xs = _outproj(xs, mx, mc, w_out_odd[i].astype(BF16), n_ctx_tiles, att)
        xs = _peer(xs, mx, mc, g_ffn[layer][None, :], peer_wq[layer].T.astype(BF16),
                   peer_keys[layer].astype(BF16), peer_u[layer].astype(BF16),
                   peer_v[layer].T.astype(BF16), n_ctx)
    return xs[:, n_ctx:, :]
```

```python
import functools
import math

import numpy as np
import jax
import jax.numpy as jnp
from jax import lax
from jax.experimental import pallas as pl
from jax.experimental.pallas import tpu as pltpu

F32 = jnp.float32
BF16 = jnp.bfloat16

EPS = 1e-6
GRID_W = 64
HEAD_DIM = 64
ROPE_FREQS = HEAD_DIM // 4
ROPE_THETA = 10000.0
A_HEADS = 4
B_HEADS = 4
B_DK = 64
B_DV = 128
GATE_RANK = 16
GATE_TAU = 16.0
GLA_CHUNK = 64
C_Q_HEADS = 16
C_KV_HEADS = 4
PEER_HEADS = 8
PEER_NKEYS = 128
PEER_DK = 128
PEER_TOPK = 16

LANES = 128
SUBLANES = 8
VMEM_LIMIT = 56 * 1024 * 1024
NEG_BIG = -3.0e38

TOK_TILE = 256
PEER_TOK_TILE = 768
PEER_LAST_TOK_TILE = 512
PEER_EXPERT_BLOCK = 1024


def _dot(a, b):
    return jnp.dot(a, b, preferred_element_type=F32)


def _dot_nt(a, b):
    return lax.dot_general(a, b, (((1,), (1,)), ((), ())), preferred_element_type=F32)


def _split_dot(onehot_bf16, x, parts):
    acc = None
    rem = x
    for _ in range(parts):
        piece = rem.astype(BF16)
        rem = rem - piece.astype(F32)
        t = _dot(onehot_bf16, piece)
        acc = t if acc is None else acc + t
    return acc


def _split_dot_right(x, onehot_bf16, parts):
    acc = None
    rem = x
    for _ in range(parts):
        piece = rem.astype(BF16)
        rem = rem - piece.astype(F32)
        t = _dot(piece, onehot_bf16)
        acc = t if acc is None else acc + t
    return acc


def _rms_mod(x, g, shift, scale):
    ms = jnp.mean(x * x, axis=-1, keepdims=True)
    y = x * lax.rsqrt(ms + EPS) * g
    return y * (1.0 + scale) + shift


def _pick_mod(mx_ref, mc_ref, k, is_ctx):
    return jnp.where(is_ctx, mc_ref[k:k + 1, :], mx_ref[0, k:k + 1, :])


def _mod_kernel(c_ref, w_ref, b_ref, o_ref):
    c = c_ref[...]
    s = c / (1.0 + jnp.exp(-c))
    o_ref[0] = _dot(s, w_ref[0]) + b_ref[0]


def _modulation(cc, w_mod, b_mod):
    depth, d, n = w_mod.shape
    rows = cc.shape[0]
    tn = d
    return pl.pallas_call(
        _mod_kernel,
        grid=(depth, n // tn),
        in_specs=[
            pl.BlockSpec((rows, d), lambda l, j: (0, 0)),
            pl.BlockSpec((1, d, tn), lambda l, j: (l, 0, j)),
            pl.BlockSpec((1, 1, tn), lambda l, j: (l, 0, j)),
        ],
        out_specs=pl.BlockSpec((1, rows, tn), lambda l, j: (l, 0, j)),
        out_shape=jax.ShapeDtypeStruct((depth, rows, n), F32),
        name="adaln_mod",
        compiler_params=pltpu.CompilerParams(
            dimension_semantics=("arbitrary", "arbitrary"), vmem_limit_bytes=VMEM_LIMIT),
    )(cc, w_mod, b_mod.reshape(depth, 1, n))


def _head_norm_rope(t, seg, gain, cos, sin_signed, first_half):
    outs = []
    for c0 in range(0, t.shape[1], 512):
        tc = t[:, c0:c0 + 512]
        ss = _split_dot_right(tc * tc, seg, 2)
        tn = tc * lax.rsqrt(ss * (1.0 / HEAD_DIM) + EPS) * gain[:, c0:c0 + 512]
        for c in range(0, 512, LANES):
            b = tn[:, c:c + LANES]
            fwd = pltpu.roll(b, LANES - ROPE_FREQS, axis=1)
            bwd = pltpu.roll(b, ROPE_FREQS, axis=1)
            partner = jnp.where(first_half, fwd, bwd)
            outs.append(b * cos + partner * sin_signed)
    return jnp.concatenate(outs, axis=1)


def _inproj_kernel(*refs, n_ctx_tiles, n_norm, n_lowp, gla):
    if gla:
        (x_ref, mx_ref, mc_ref, g_ref, w_ref, gain_ref, cos_ref, sin_ref, seg_ref,
         wg_ref, waf_ref, wab_ref, bf_ref, bb_ref, qkv_ref, gl_ref) = refs
    else:
        (x_ref, mx_ref, mc_ref, g_ref, w_ref, gain_ref, cos_ref, sin_ref, seg_ref,
         qkv_ref) = refs
    is_ctx = pl.program_id(1) < n_ctx_tiles
    shift = _pick_mod(mx_ref, mc_ref, 0, is_ctx)
    scale = _pick_mod(mx_ref, mc_ref, 1, is_ctx)
    a = _rms_mod(x_ref[0], g_ref[...], shift, scale).astype(BF16)
    p = _dot(a, w_ref[...])
    lane = lax.broadcasted_iota(jnp.int32, (1, LANES), 1)
    first_half = (lane % (2 * ROPE_FREQS)) < ROPE_FREQS
    qk = _head_norm_rope(p[:, :n_norm], seg_ref[...], gain_ref[...], cos_ref[...], sin_ref[...],
                         first_half)
    qkv_ref[0, :, :n_norm] = qk.astype(BF16)
    qkv_ref[0, :, n_norm:] = p[:, n_norm:n_lowp].astype(BF16)
    if gla:
        nq = B_HEADS * B_DK
        rest = p[:, n_lowp:]
        gl_ref[0, :, :nq] = rest[:, :nq] * (B_DK ** -0.5)
        gl_ref[0, :, nq:rest.shape[1]] = rest[:, nq:]
        pg = _dot(a, wg_ref[...]).astype(BF16)
        for wref, bref, off in ((waf_ref, bf_ref, 0), (wab_ref, bb_ref, nq)):
            z = _dot(pg, wref[...]) + bref[...]
            la = (jnp.minimum(z, 0.0) - jnp.log(1.0 + jnp.exp(-jnp.abs(z)))) * (1.0 / GATE_TAU)
            gl_ref[0, :, rest.shape[1] + off:rest.shape[1] + off + nq] = la


def _inproj(xs, mx, mc, g, w, gain, cos_t, sin_t, seg, n_ctx_tiles, n_norm, gla_ops=None):
    b, l, d = xs.shape
    tm = TOK_TILE
    n = w.shape[1]
    gla = gla_ops is not None
    n_lowp = n - (2 * B_HEADS * B_DK + 2 * B_HEADS * B_DV if gla else 0)
    full = lambda shape: pl.BlockSpec(shape, lambda i, j: (0,) * len(shape))
    in_specs = [
        pl.BlockSpec((1, tm, d), lambda i, j: (i, j, 0)),
        pl.BlockSpec((1, 8, d), lambda i, j: (i, 0, 0)),
        full((8, d)), full((1, d)), full((d, n)), full((1, n_norm)),
        pl.BlockSpec((tm, LANES), lambda i, j: (j, 0)),
        pl.BlockSpec((tm, LANES), lambda i, j: (j, 0)),
        full((512, 512)),
    ]
    ops = [xs, mx, mc, g, w, gain, cos_t, sin_t, seg]
    out_shape = [jax.ShapeDtypeStruct((b, l, n_lowp), BF16)]
    out_specs = [pl.BlockSpec((1, tm, n_lowp), lambda i, j: (i, j, 0))]
    if gla:
        wg, waf, wab, bf, bb = gla_ops
        in_specs += [full(wg.shape), full(waf.shape), full(wab.shape), full(bf.shape), full(bb.shape)]
        ops += [wg, waf, wab, bf, bb]
        n_gl = (n - n_lowp) + 2 * B_HEADS * B_DK
        out_shape.append(jax.ShapeDtypeStruct((b, l, n_gl), F32))
        out_specs.append(pl.BlockSpec((1, tm, n_gl), lambda i, j: (i, j, 0)))
    return pl.pallas_call(
        functools.partial(_inproj_kernel, n_ctx_tiles=n_ctx_tiles, n_norm=n_norm, n_lowp=n_lowp, gla=gla),
        grid=(b, l // tm),
        in_specs=in_specs,
        out_specs=out_specs,
        out_shape=out_shape,
        name="inproj_even" if gla else "inproj_odd",
        compiler_params=pltpu.CompilerParams(
            dimension_semantics=("parallel", "arbitrary"), vmem_limit_bytes=VMEM_LIMIT),
    )(*ops)


def _softmax_parts(s):
    m = jnp.max(s, axis=-1, keepdims=True)
    e = jnp.exp2(s - m)
    return e, 1.0 / jnp.sum(e, axis=-1, keepdims=True)


def _diff_body(q, k, v, lam, subln, post_scale):
    lane = lax.broadcasted_iota(jnp.int32, q.shape, 1)
    lo = lane < HEAD_DIM
    zero = jnp.zeros_like(q)
    e0, r0 = _softmax_parts(_dot_nt(jnp.where(lo, q, zero), k))
    e1, r1 = _softmax_parts(_dot_nt(jnp.where(lo, zero, q), k))
    w = e0 * r0 - e1 * (r1 * lam)
    o = _dot(w.astype(BF16), v)
    ms = jnp.mean(o * o, axis=-1, keepdims=True)
    return o * lax.rsqrt(ms + EPS) * subln * post_scale


def _diff_attn_kernel(q_ref, k_ref, v_ref, lam_ref, subln_ref, o_ref, *, n_ctx_tiles, ctx_len, lam_init):
    lv = lam_ref[...]
    lam = (jnp.exp(jnp.sum(lv[0:1] * lv[1:2], axis=-1, keepdims=True))
           - jnp.exp(jnp.sum(lv[2:3] * lv[3:4], axis=-1, keepdims=True)) + lam_init)
    is_ctx = pl.program_id(2) < n_ctx_tiles

    @pl.when(is_ctx)
    def _():
        o_ref[0] = _diff_body(q_ref[0], k_ref[0, :ctx_len], v_ref[0, :ctx_len], lam, subln_ref[...],
                              1.0 - lam_init).astype(o_ref.dtype)

    @pl.when(jnp.logical_not(is_ctx))
    def _():
        o_ref[0] = _diff_body(q_ref[0], k_ref[0], v_ref[0], lam, subln_ref[...],
                              1.0 - lam_init).astype(o_ref.dtype)


def _diff_attn(qkv, lam_vec, subln, n_ctx_tiles, ctx_len, lam_init):
    b, l, _ = qkv.shape
    tq = TOK_TILE
    w = 2 * HEAD_DIM
    return pl.pallas_call(
        functools.partial(_diff_attn_kernel, n_ctx_tiles=n_ctx_tiles, ctx_len=ctx_len, lam_init=lam_init),
        grid=(b, A_HEADS, l // tq),
        in_specs=[
            pl.BlockSpec((1, tq, w), lambda i, h, j: (i, j, h)),
            pl.BlockSpec((1, l, w), lambda i, h, j: (i, 0, A_HEADS + h)),
            pl.BlockSpec((1, l, w), lambda i, h, j: (i, 0, 2 * A_HEADS + h)),
            pl.BlockSpec((4, HEAD_DIM), lambda i, h, j: (0, 0)),
            pl.BlockSpec((1, w), lambda i, h, j: (0, 0)),
        ],
        out_specs=pl.BlockSpec((1, tq, w), lambda i, h, j: (i, j, h)),
        out_shape=jax.ShapeDtypeStruct((b, l, A_HEADS * w), BF16),
        name="diff_attn",
        compiler_params=pltpu.CompilerParams(
            dimension_semantics=("parallel", "parallel", "arbitrary"), vmem_limit_bytes=VMEM_LIMIT),
    )(qkv, qkv, qkv, lam_vec, subln)


def _gqa_body(q, k, v, o_ref):
    group = C_Q_HEADS // C_KV_HEADS
    for kv in range(C_KV_HEADS):
        kd = k[:, kv * LANES:(kv + 1) * LANES]
        vd = v[:, kv * LANES:(kv + 1) * LANES]
        for a in range(group // 2):
            c0 = (kv * group + 2 * a) * HEAD_DIM
            qp = q[:, c0:c0 + LANES]
            lane = lax.broadcasted_iota(jnp.int32, qp.shape, 1)
            lo = lane < HEAD_DIM
            zero = jnp.zeros_like(qp)
            e0, r0 = _softmax_parts(_dot_nt(jnp.where(lo, qp, zero), kd))
            e1, r1 = _softmax_parts(_dot_nt(jnp.where(lo, zero, qp), kd))
            o0 = _dot(e0.astype(BF16), vd) * r0
            o1 = _dot(e1.astype(BF16), vd) * r1
            o_ref[0, :, c0:c0 + LANES] = jnp.where(lo, o0, o1).astype(o_ref.dtype)


def _gqa_attn_kernel(q_ref, k_ref, v_ref, o_ref, *, n_ctx_tiles, ctx_len):
    is_ctx = pl.program_id(1) < n_ctx_tiles

    @pl.when(is_ctx)
    def _():
        _gqa_body(q_ref[0], k_ref[0, :ctx_len], v_ref[0, :ctx_len], o_ref)

    @pl.when(jnp.logical_not(is_ctx))
    def _():
        _gqa_body(q_ref[0], k_ref[0], v_ref[0], o_ref)


def _gqa_attn(qkv, n_ctx_tiles, ctx_len):
    b, l, _ = qkv.shape
    tq = TOK_TILE
    wq = C_Q_HEADS * HEAD_DIM
    wk = 2 * C_KV_HEADS * HEAD_DIM
    return pl.pallas_call(
        functools.partial(_gqa_attn_kernel, n_ctx_tiles=n_ctx_tiles, ctx_len=ctx_len),
        grid=(b, l // tq),
        in_specs=[
            pl.BlockSpec((1, tq, wq), lambda i, j: (i, j, 0)),
            pl.BlockSpec((1, l, wk), lambda i, j: (i, 0, wq // wk)),
            pl.BlockSpec((1, l, wk), lambda i, j: (i, 0, wq // wk + 1)),
        ],
        out_specs=pl.BlockSpec((1, tq, wq), lambda i, j: (i, j, 0)),
        out_shape=jax.ShapeDtypeStruct((b, l, wq), BF16),
        name="gqa_attn",
        compiler_params=pltpu.CompilerParams(
            dimension_semantics=("parallel", "arbitrary"), vmem_limit_bytes=VMEM_LIMIT),
    )(qkv, qkv, qkv)


def _gla_tile(q, k, v, la, st_ref, reverse):
    t = q.shape[0]
    row = lax.broadcasted_iota(jnp.int32, (t, t), 0)
    col = lax.broadcasted_iota(jnp.int32, (t, t), 1)
    same = jnp.where((row // GLA_CHUNK) == (col // GLA_CHUNK), 1.0, 0.0)
    tri = jnp.where((col >= row) if reverse else (col <= row), same, 0.0)
    cum = _split_dot(tri.astype(BF16), la, 3)
    tot = _split_dot(same.astype(BF16), la, 3)
    q_dec = q * jnp.exp(cum)
    k_inv = (k * jnp.exp(-cum)).astype(BF16)
    k_end = (k * jnp.exp(tot - cum)).astype(BF16)
    lane = lax.broadcasted_iota(jnp.int32, q.shape, 1)
    lo = lane < B_DK
    qz = jnp.zeros_like(q_dec)
    vb = v.astype(BF16)
    keep = tri > 0.5
    a_lo = jnp.where(keep, _dot_nt(jnp.where(lo, q_dec, qz).astype(BF16), k_inv), 0.0).astype(BF16)
    a_hi = jnp.where(keep, _dot_nt(jnp.where(lo, qz, q_dec).astype(BF16), k_inv), 0.0).astype(BF16)
    o_intra = jnp.concatenate([_dot(a_lo, vb[:, :B_DV]), _dot(a_hi, vb[:, B_DV:])], axis=1)
    qd = q_dec.astype(BF16)
    srow = lax.broadcasted_iota(jnp.int32, (2 * B_DV, 2 * B_DK), 0)
    scol = lax.broadcasted_iota(jnp.int32, (2 * B_DV, 2 * B_DK), 1)
    own = (srow < B_DV) == (scol < B_DK)
    n_chunks = t // GLA_CHUNK
    outs = [None] * n_chunks
    for c in (range(n_chunks - 1, -1, -1) if reverse else range(n_chunks)):
        r0 = c * GLA_CHUNK
        st = st_ref[...]
        outs[c] = o_intra[r0:r0 + GLA_CHUNK] + _dot_nt(qd[r0:r0 + GLA_CHUNK], st.astype(BF16))
        decay = jnp.exp(tot[r0:r0 + 1, :])
        d_state = _dot(v[r0:r0 + GLA_CHUNK].T.astype(BF16), k_end[r0:r0 + GLA_CHUNK])
        st_ref[...] = st * decay + jnp.where(own, d_state, 0.0)
    return jnp.concatenate(outs, axis=0)


def _gla_kernel(qf, kf, vf, lf, qb, kb, vb, lb, of_ref, ob_ref, stf_ref, stb_ref):
    @pl.when(pl.program_id(2) == 0)
    def _():
        stf_ref[...] = jnp.zeros_like(stf_ref)
        stb_ref[...] = jnp.zeros_like(stb_ref)

    of_ref[0] = _gla_tile(qf[0], kf[0], vf[0], lf[0], stf_ref, False)
    ob_ref[0] = _gla_tile(qb[0], kb[0], vb[0], lb[0], stb_ref, True)


def _gla(gl, n_ctx_tiles):
    b, l, _ = gl.shape
    t = TOK_TILE
    nt = l // t
    pairs = B_HEADS // 2
    wk = 2 * B_DK
    wv = 2 * B_DV
    nq = B_HEADS * B_DK
    la_f0 = (2 * nq + 2 * B_HEADS * B_DV) // wk

    def bwd(j):
        return jnp.where(j < n_ctx_tiles, n_ctx_tiles - 1 - j, nt - 1 - (j - n_ctx_tiles))

    def specs(tile):
        return [
            pl.BlockSpec((1, t, wk), lambda i, p, j: (i, tile(j), p)),
            pl.BlockSpec((1, t, wk), lambda i, p, j: (i, tile(j), nq // wk + p)),
            pl.BlockSpec((1, t, wv), lambda i, p, j: (i, tile(j), 2 * nq // wv + p)),
        ]

    fwd = lambda j: j
    in_specs = (specs(fwd) + [pl.BlockSpec((1, t, wk), lambda i, p, j: (i, j, la_f0 + p))]
                + specs(bwd) + [pl.BlockSpec((1, t, wk), lambda i, p, j: (i, bwd(j), la_f0 + nq // wk + p))])
    return pl.pallas_call(
        _gla_kernel,
        grid=(b, pairs, nt),
        in_specs=in_specs,
        out_specs=[pl.BlockSpec((1, t, wv), lambda i, p, j: (i, j, p)),
                   pl.BlockSpec((1, t, wv), lambda i, p, j: (i, bwd(j), p))],
        out_shape=[jax.ShapeDtypeStruct((b, l, B_HEADS * B_DV), F32)] * 2,
        scratch_shapes=[pltpu.VMEM((wv, wk), F32), pltpu.VMEM((wv, wk), F32)],
        name="gla",
        compiler_params=pltpu.CompilerParams(
            dimension_semantics=("parallel", "parallel", "arbitrary"), vmem_limit_bytes=VMEM_LIMIT),
    )(*([gl] * 8))


def _outproj_even_kernel(x_ref, mx_ref, mc_ref, d_ref, of_ref, ob_ref, r_ref, gain_ref, w_ref, o_ref,
                         *, n_ctx_tiles):
    is_ctx = pl.program_id(1) < n_ctx_tiles
    gate = _pick_mod(mx_ref, mc_ref, 2, is_ctx)
    o = of_ref[0] + ob_ref[0]
    r = r_ref[0]
    parts = []
    for c in range(0, o.shape[1], B_DV):
        oc = o[:, c:c + B_DV]
        ms = jnp.mean(oc * oc, axis=-1, keepdims=True)
        parts.append(oc * lax.rsqrt(ms + EPS))
    g = jnp.concatenate(parts, axis=1) * gain_ref[...] * (r / (1.0 + jnp.exp(-r)))
    nd = d_ref.shape[2]
    y = _dot(d_ref[0], w_ref[:nd, :]) + _dot(g.astype(BF16), w_ref[nd:, :])
    o_ref[0] = x_ref[0] + gate * y


def _outproj_odd_kernel(x_ref, mx_ref, mc_ref, a_ref, w_ref, o_ref, *, n_ctx_tiles):
    is_ctx = pl.program_id(1) < n_ctx_tiles
    gate = _pick_mod(mx_ref, mc_ref, 2, is_ctx)
    o_ref[0] = x_ref[0] + gate * _dot(a_ref[0], w_ref[...])


def _outproj(xs, mx, mc, w, n_ctx_tiles, attn, gla_ops=None):
    b, l, d = xs.shape
    tm = TOK_TILE
    tok = lambda width, blk=0: pl.BlockSpec((1, tm, width), lambda i, j: (i, j, blk))
    full = lambda shape: pl.BlockSpec(shape, lambda i, j: (0,) * len(shape))
    head = [tok(d), pl.BlockSpec((1, 8, d), lambda i, j: (i, 0, 0)), full((8, d))]
    if gla_ops is not None:
        o_f, o_b, gl, gain = gla_ops
        wv = B_HEADS * B_DV
        r_blk = (2 * B_HEADS * B_DK + wv) // wv
        kern = functools.partial(_outproj_even_kernel, n_ctx_tiles=n_ctx_tiles)
        in_specs = head + [tok(attn.shape[2]), tok(wv), tok(wv), tok(wv, r_blk), full((1, wv)), full(w.shape)]
        ops = [xs, mx, mc, attn, o_f, o_b, gl, gain, w]
    else:
        kern = functools.partial(_outproj_odd_kernel, n_ctx_tiles=n_ctx_tiles)
        in_specs = head + [tok(attn.shape[2]), full(w.shape)]
        ops = [xs, mx, mc, attn, w]
    return pl.pallas_call(
        kern,
        grid=(b, l // tm),
        in_specs=in_specs,
        out_specs=tok(d),
        out_shape=jax.ShapeDtypeStruct((b, l, d), F32),
        name="outproj_even" if gla_ops is not None else "outproj_odd",
        compiler_params=pltpu.CompilerParams(
            dimension_semantics=("parallel", "arbitrary"), vmem_limit_bytes=VMEM_LIMIT),
    )(*ops)


def _ce(v, i, j):
    a, b = v[i], v[j]
    v[i] = jnp.maximum(a, b)
    v[j] = jnp.minimum(a, b)


def _bitonic_sort_desc(v):
    n = len(v)
    k = 2
    while k <= n:
        j = k // 2
        while j >= 1:
            for i in range(n):
                m = i ^ j
                if m > i:
                    if (i & k) == 0:
                        _ce(v, i, m)
                    else:
                        _ce(v, m, i)
            j //= 2
        k *= 2


def _bitonic_merge_desc(v):
    n = len(v)
    j = n // 2
    while j >= 1:
        for i in range(n):
            m = i ^ j
            if m > i:
                _ce(v, i, m)
        j //= 2


def _merge_top(a, b):
    n = len(a)
    c = [jnp.maximum(a[i], b[n - 1 - i]) for i in range(n)]
    _bitonic_merge_desc(c)
    return c


def _top_sorted(s):
    groups = s.shape[0] // SUBLANES
    v = [s[g * SUBLANES:(g + 1) * SUBLANES, :] for g in range(groups)]
    _bitonic_sort_desc(v)
    v = v[:PEER_TOPK]
    shift = SUBLANES // 2
    while shift >= 1:
        v = _merge_top(v, [pltpu.roll(x, shift, axis=0) for x in v])
        shift //= 2
    return v


def _gelu_tanh(x):
    c0 = 0.7978845608028654
    u = x * (c0 + (c0 * 0.044715) * (x * x))
    return x * (0.5 + 0.5 * jnp.tanh(u))


def _prefix_count(hits):
    n = None
    for k, hit in enumerate(hits):
        n = jnp.where(hit, float(k + 1), 0.0 if n is None else n)
    return n


def _routing_weights(e1_ref, cnt_ref, e2_ref, rk_ref, i):
    tm = e1_ref.shape[2]
    pack = 2 * SUBLANES
    cnts = [jnp.broadcast_to(cnt_ref[hd, pl.ds(i, 1), :], (pack, tm)).astype(BF16)
            for hd in range(PEER_HEADS)]
    e1s = [jnp.broadcast_to(e1_ref[hd, pl.ds(i, 1), :], (pack, tm)).astype(BF16)
           for hd in range(PEER_HEADS)]
    cols = []
    for c0 in range(0, tm, LANES):
        g = jnp.zeros((PEER_NKEYS // pack, pack, LANES), BF16)
        for hd in range(PEER_HEADS):
            cnt = cnts[hd][:, c0:c0 + LANES]
            e1 = e1s[hd][:, c0:c0 + LANES]
            rk = rk_ref[hd, :, c0:c0 + LANES].reshape(g.shape)
            e2 = e2_ref[hd, :, c0:c0 + LANES].reshape(g.shape)
            g = g + jnp.where(rk < cnt[None], e1[None] * e2, jnp.zeros_like(e2))
        cols.append(g.reshape(PEER_NKEYS, LANES))
    return jnp.concatenate(cols, axis=1)


def _peer_kernel(x_ref, mx_ref, mc_ref, g_ref, wqt_ref, keys_ref, u_ref, vt_ref, o_ref,
                 ht_ref, qt_ref, acc_ref, e1_ref, cnt_ref, e2_ref, rk_ref,
                 *, ctx_len, rows_i, n_blocks):
    tm = x_ref.shape[1]
    step = pl.program_id(2)
    row = pl.program_id(1) * tm + lax.broadcasted_iota(jnp.int32, (tm, 1), 0)
    is_ctx = row < ctx_len
    half = PEER_DK // 2

    def scores(hd):
        s1 = _dot(keys_ref[hd, 0], qt_ref[hd * PEER_DK:hd * PEER_DK + half, :])
        s2 = _dot(keys_ref[hd, 1], qt_ref[hd * PEER_DK + half:(hd + 1) * PEER_DK, :])
        return s1, s2

    @pl.when(step == 0)
    def _():
        shift = jnp.where(is_ctx, mc_ref[3:4, :], mx_ref[0, 3:4, :])
        scale = jnp.where(is_ctx, mc_ref[4:5, :], mx_ref[0, 4:5, :])
        h = _rms_mod(x_ref[0], g_ref[...], shift, scale)
        ht = h.T.astype(BF16)
        ht_ref[...] = ht
        acc_ref[...] = jnp.zeros_like(acc_ref)
        qt_ref[...] = _dot(wqt_ref[...], ht).astype(BF16)
        sub = lax.broadcasted_iota(jnp.int32, (SUBLANES, tm), 0)
        a_all = [jnp.zeros((SUBLANES, tm), F32)] * PEER_TOPK
        b_all = [jnp.zeros((SUBLANES, tm), F32)] * PEER_TOPK
        for hd in range(PEER_HEADS):
            s1, s2 = scores(hd)
            ta = _top_sorted(s1)
            tb = _top_sorted(s2)
            a_all = [jnp.where(sub == hd, ta[k], a_all[k]) for k in range(PEER_TOPK)]
            b_all = [jnp.where(sub == hd, tb[k], b_all[k]) for k in range(PEER_TOPK)]
        neg = jnp.full((SUBLANES, tm), NEG_BIG, F32)
        top = [a_all[0] + b_all[c] for c in range(PEER_TOPK)]
        for r in range(1, PEER_TOPK):
            n_r = PEER_TOPK // (r + 1)
            top = _merge_top(top, [a_all[r] + b_all[c] for c in range(n_r)] + [neg] * (PEER_TOPK - n_r))
        z = jnp.zeros((SUBLANES, tm), F32)
        for k in range(PEER_TOPK):
            z = z + jnp.exp(top[k] - top[0])
        thr = top[PEER_TOPK - 1]
        inv_z = 1.0 / z
        big = jnp.full((SUBLANES, tm), -NEG_BIG, F32)
        theta = []
        for k in range(PEER_TOPK):
            t = big
            for r in range(PEER_TOPK // (k + 1)):
                t = jnp.minimum(t, jnp.where(a_all[r] + b_all[k] >= thr, a_all[r], big))
            theta.append(t)
        for hd in range(PEER_HEADS):
            s1, s2 = scores(hd)
            cnt_ref[hd] = _prefix_count([s1 >= t[hd:hd + 1, :] for t in theta])
            rk_ref[hd] = _prefix_count([s2 < b[hd:hd + 1, :] for b in b_all]).astype(BF16)
            e1_ref[hd] = jnp.exp(s1 - a_all[0][hd:hd + 1, :]) * inv_z[hd:hd + 1, :]
            e2_ref[hd] = jnp.exp(s2 - b_all[0][hd:hd + 1, :]).astype(BF16)

    gs = [_routing_weights(e1_ref, cnt_ref, e2_ref, rk_ref, step * rows_i + ii) for ii in range(rows_i)]
    at = _dot(u_ref[...], ht_ref[...])
    wt = [_gelu_tanh(at[ii * PEER_NKEYS:(ii + 1) * PEER_NKEYS, :]).astype(BF16) * gs[ii]
          for ii in range(rows_i)]
    acc_ref[...] += _dot(vt_ref[...], jnp.concatenate(wt, axis=0))

    @pl.when(step == n_blocks - 1)
    def _():
        gate = jnp.where(is_ctx, mc_ref[5:6, :], mx_ref[0, 5:6, :])
        o_ref[0] = x_ref[0] + gate * acc_ref[...].T


def _peer(xs, mx, mc, g, wqt, keys, u, vt, ctx_len, tm):
    b, l, d = xs.shape
    eblk = PEER_EXPERT_BLOCK
    n_blocks = u.shape[0] // eblk
    full = lambda shape: pl.BlockSpec(shape, lambda i, j, e: (0,) * len(shape))
    sc = (PEER_HEADS, PEER_NKEYS, tm)
    return pl.pallas_call(
        functools.partial(_peer_kernel, ctx_len=ctx_len, rows_i=eblk // PEER_NKEYS, n_blocks=n_blocks),
        grid=(b, l // tm, n_blocks),
        in_specs=[
            pl.BlockSpec((1, tm, d), lambda i, j, e: (i, j, 0)),
            pl.BlockSpec((1, 8, d), lambda i, j, e: (i, 0, 0)),
            full((8, d)), full((1, d)),
            pl.BlockSpec(wqt.shape, lambda i, j, e: (0, 0), pipeline_mode=pl.Buffered(1)),
            full(keys.shape),
            pl.BlockSpec((eblk, d), lambda i, j, e: (e, 0)),
            pl.BlockSpec((d, eblk), lambda i, j, e: (0, e)),
        ],
        out_specs=pl.BlockSpec((1, tm, d), lambda i, j, e: (i, j, 0)),
        out_shape=jax.ShapeDtypeStruct((b, l, d), F32),
        scratch_shapes=[
            pltpu.VMEM((d, tm), BF16), pltpu.VMEM((PEER_HEADS * PEER_DK, tm), BF16),
            pltpu.VMEM((d, tm), F32),
            pltpu.VMEM(sc, F32), pltpu.VMEM(sc, F32), pltpu.VMEM(sc, BF16), pltpu.VMEM(sc, BF16),
        ],
        name="peer",
        compiler_params=pltpu.CompilerParams(
            dimension_semantics=("parallel", "parallel", "arbitrary"), vmem_limit_bytes=VMEM_LIMIT),
    )(xs, mx, mc, g, wqt, keys, u, vt)


def kernel(x, c, ctx, c_ctx, w_mod, b_mod, g_mix, g_ffn, w_in_even, w_out_even, a_q_gain, a_k_gain,
           a_lambda, a_subln, b_w_af, b_b_af, b_w_ab, b_b_ab, b_gain, w_in_odd, w_out_odd, c_q_gain,
           c_k_gain, peer_wq, peer_keys, peer_u, peer_v):
    bsz, seq, d = x.shape
    n_ctx = ctx.shape[1]
    depth = w_mod.shape[0]
    n_ctx_tiles = n_ctx // TOK_TILE
    assert n_ctx % TOK_TILE == 0 and seq % TOK_TILE == 0 and (n_ctx + seq) % PEER_TOK_TILE == 0
    assert seq % PEER_LAST_TOK_TILE == 0

    rows = seq // GRID_W
    row = jnp.repeat(jnp.arange(rows, dtype=F32), GRID_W)
    col = jnp.tile(jnp.arange(GRID_W, dtype=F32), rows)
    inv_freq = ROPE_THETA ** (-jnp.arange(ROPE_FREQS, dtype=F32) / ROPE_FREQS)
    ang = jnp.stack([row, col], axis=-1)[:, :, None] * inv_freq
    cos = jnp.cos(ang)
    sin = jnp.sin(ang)
    cos64 = jnp.concatenate([cos[:, 0], cos[:, 0], cos[:, 1], cos[:, 1]], axis=-1)
    sin64 = jnp.concatenate([-sin[:, 0], sin[:, 0], -sin[:, 1], sin[:, 1]], axis=-1)
    cos_t = jnp.concatenate([jnp.ones((n_ctx, LANES), F32), jnp.tile(cos64, (1, 2))], axis=0)
    sin_t = jnp.concatenate([jnp.zeros((n_ctx, LANES), F32), jnp.tile(sin64, (1, 2))], axis=0)
    seg = jnp.asarray(np.kron(np.eye(512 // HEAD_DIM), np.ones((HEAD_DIM, HEAD_DIM))), BF16)

    pad = (-(bsz + 1)) % SUBLANES
    cc = jnp.concatenate([c, c_ctx[None, :], jnp.zeros((pad, d), F32)], axis=0)
    mod = _modulation(cc, w_mod, b_mod)
    mod = mod.reshape(depth, bsz + 1 + pad, 6, d)
    mod = jnp.concatenate([mod, jnp.zeros((depth, bsz + 1 + pad, 2, d), F32)], axis=2)

    xs = jnp.concatenate([ctx, x], axis=1)
    scale = HEAD_DIM ** -0.5 * math.log2(math.e)
    for layer in range(depth):
        mx = mod[layer, :bsz]
        mc = mod[layer, bsz]
        i = layer // 2
        g1 = g_mix[layer][None, :]
        if layer % 2 == 0:
            lam_init = 0.8 - 0.6 * math.exp(-0.3 * layer)
            w = w_in_even[i]
            n_main = w.shape[1] - 2 * GATE_RANK
            gain = jnp.concatenate([jnp.tile(a_q_gain[i] * scale, 2 * A_HEADS),
                                    jnp.tile(a_k_gain[i], 2 * A_HEADS)])[None, :]
            wg = jnp.pad(w[:, n_main:], ((0, 0), (0, LANES - 2 * GATE_RANK))).astype(BF16)
            waf = jnp.pad(b_w_af[i], ((0, LANES - GATE_RANK), (0, 0))).astype(BF16)
            wab = jnp.pad(b_w_ab[i], ((GATE_RANK, LANES - 2 * GATE_RANK), (0, 0))).astype(BF16)
            qkv, gl = _inproj(xs, mx, mc, g1, w[:, :n_main].astype(BF16), gain, cos_t, sin_t, seg,
                              n_ctx_tiles, gain.shape[1],
                              gla_ops=(wg, waf, wab, b_b_af[i][None, :], b_b_ab[i][None, :]))
            dx = _diff_attn(qkv, a_lambda[i], a_subln[i][None, :], n_ctx_tiles, n_ctx, lam_init)
            o_f, o_b = _gla(gl, n_ctx_tiles)
            xs = _outproj(xs, mx, mc, w_out_even[i].astype(BF16), n_ctx_tiles, dx,
                          gla_ops=(o_f, o_b, gl, jnp.tile(b_gain[i], B_HEADS)[None, :]))
        else:
            w = w_in_odd[i]
            nq = C_Q_HEADS * HEAD_DIM
            nk = C_KV_HEADS * HEAD_DIM
            dup = lambda m: jnp.repeat(m.reshape(d, C_KV_HEADS, 1, HEAD_DIM), 2, axis=2).reshape(d, 2 * nk)
            w2 = jnp.concatenate([w[:, :nq], dup(w[:, nq:nq + nk]), dup(w[:, nq + nk:])], axis=1)
            gain = jnp.concatenate([jnp.tile(c_q_gain[i] * scale, C_Q_HEADS),
                                    jnp.tile(c_k_gain[i], 2 * C_KV_HEADS)])[None, :]
            qkv = _inproj(xs, mx, mc, g1, w2.astype(BF16), gain, cos_t, sin_t, seg,
                          n_ctx_tiles, gain.shape[1])[0]
            att = _gqa_attn(qkv, n_ctx_tiles, n_ctx)
            xs = _outproj(xs, mx, mc, w_out_odd[i].astype(BF16), n_ctx_tiles, att)
        last = layer == depth - 1
        if last:
            xs, peer_ctx, peer_tile = xs[:, n_ctx:, :], 0, PEER_LAST_TOK_TILE
        else:
            peer_ctx, peer_tile = n_ctx, PEER_TOK_TILE
        xs = _peer(xs, mx, mc, g_ffn[layer][None, :], peer_wq[layer].T.astype(BF16),
                   peer_keys[layer].astype(BF16), peer_u[layer].astype(BF16),
                   peer_v[layer].T.astype(BF16), peer_ctx, peer_tile)
    return xs
```

```python
import functools
import math

import numpy as np
import jax
import jax.numpy as jnp
from jax import lax
from jax.experimental import pallas as pl
from jax.experimental.pallas import tpu as pltpu

F32 = jnp.float32
BF16 = jnp.bfloat16

EPS = 1e-6
GRID_W = 64
HEAD_DIM = 64
ROPE_FREQS = HEAD_DIM // 4
ROPE_THETA = 10000.0
A_HEADS = 4
B_HEADS = 4
B_DK = 64
B_DV = 128
GATE_RANK = 16
GATE_TAU = 16.0
GLA_CHUNK = 64
C_Q_HEADS = 16
C_KV_HEADS = 4
PEER_HEADS = 8
PEER_NKEYS = 128
PEER_DK = 128
PEER_TOPK = 16

LANES = 128
SUBLANES = 8
VMEM_LIMIT = 56 * 1024 * 1024
NEG_BIG = -3.0e38

TOK_TILE = 256
PEER_TOK_TILE = 768
PEER_EXPERT_BLOCK = 1024
PEER_ROW_GROUP = 2


def _dot(a, b):
    return jnp.dot(a, b, preferred_element_type=F32)


def _dot_nt(a, b):
    return lax.dot_general(a, b, (((1,), (1,)), ((), ())), preferred_element_type=F32)


def _split_dot(onehot_bf16, x, parts):
    acc = None
    rem = x
    for _ in range(parts):
        piece = rem.astype(BF16)
        rem = rem - piece.astype(F32)
        t = _dot(onehot_bf16, piece)
        acc = t if acc is None else acc + t
    return acc


def _split_dot_right(x, onehot_bf16, parts):
    acc = None
    rem = x
    for _ in range(parts):
        piece = rem.astype(BF16)
        rem = rem - piece.astype(F32)
        t = _dot(piece, onehot_bf16)
        acc = t if acc is None else acc + t
    return acc


def _rms_mod(x, g, shift, scale):
    ms = jnp.mean(x * x, axis=-1, keepdims=True)
    y = x * lax.rsqrt(ms + EPS) * g
    return y * (1.0 + scale) + shift


def _pick_mod(mx_ref, mc_ref, k, is_ctx):
    return jnp.where(is_ctx, mc_ref[k:k + 1, :], mx_ref[0, k:k + 1, :])


def _mod_kernel(c_ref, w_ref, b_ref, o_ref):
    c = c_ref[...]
    s = c / (1.0 + jnp.exp(-c))
    o_ref[0] = _dot(s, w_ref[0]) + b_ref[0]


def _modulation(cc, w_mod, b_mod):
    depth, d, n = w_mod.shape
    rows = cc.shape[0]
    tn = d
    return pl.pallas_call(
        _mod_kernel,
        grid=(depth, n // tn),
        in_specs=[
            pl.BlockSpec((rows, d), lambda l, j: (0, 0)),
            pl.BlockSpec((1, d, tn), lambda l, j: (l, 0, j)),
            pl.BlockSpec((1, 1, tn), lambda l, j: (l, 0, j)),
        ],
        out_specs=pl.BlockSpec((1, rows, tn), lambda l, j: (l, 0, j)),
        out_shape=jax.ShapeDtypeStruct((depth, rows, n), F32),
        name="adaln_mod",
        compiler_params=pltpu.CompilerParams(
            dimension_semantics=("arbitrary", "arbitrary"), vmem_limit_bytes=VMEM_LIMIT),
    )(cc, w_mod, b_mod.reshape(depth, 1, n))


def _head_norm_rope(t, seg, gain, cos, sin_signed, first_half):
    outs = []
    for c0 in range(0, t.shape[1], 512):
        tc = t[:, c0:c0 + 512]
        ss = _split_dot_right(tc * tc, seg, 2)
        tn = tc * lax.rsqrt(ss * (1.0 / HEAD_DIM) + EPS) * gain[:, c0:c0 + 512]
        for c in range(0, 512, LANES):
            b = tn[:, c:c + LANES]
            fwd = pltpu.roll(b, LANES - ROPE_FREQS, axis=1)
            bwd = pltpu.roll(b, ROPE_FREQS, axis=1)
            partner = jnp.where(first_half, fwd, bwd)
            outs.append(b * cos + partner * sin_signed)
    return jnp.concatenate(outs, axis=1)


def _inproj_kernel(*refs, n_ctx_tiles, n_norm, n_lowp, gla):
    if gla:
        (x_ref, mx_ref, mc_ref, g_ref, w_ref, gain_ref, cos_ref, sin_ref, seg_ref,
         wg_ref, waf_ref, wab_ref, bf_ref, bb_ref, qkv_ref, gl_ref) = refs
    else:
        (x_ref, mx_ref, mc_ref, g_ref, w_ref, gain_ref, cos_ref, sin_ref, seg_ref,
         qkv_ref) = refs
    is_ctx = pl.program_id(1) < n_ctx_tiles
    shift = _pick_mod(mx_ref, mc_ref, 0, is_ctx)
    scale = _pick_mod(mx_ref, mc_ref, 1, is_ctx)
    a = _rms_mod(x_ref[0], g_ref[...], shift, scale).astype(BF16)
    p = _dot(a, w_ref[...])
    lane = lax.broadcasted_iota(jnp.int32, (1, LANES), 1)
    first_half = (lane % (2 * ROPE_FREQS)) < ROPE_FREQS
    qk = _head_norm_rope(p[:, :n_norm], seg_ref[...], gain_ref[...], cos_ref[...], sin_ref[...],
                         first_half)
    qkv_ref[0, :, :n_norm] = qk.astype(BF16)
    qkv_ref[0, :, n_norm:] = p[:, n_norm:n_lowp].astype(BF16)
    if gla:
        nq = B_HEADS * B_DK
        rest = p[:, n_lowp:]
        gl_ref[0, :, :nq] = rest[:, :nq] * (B_DK ** -0.5)
        gl_ref[0, :, nq:rest.shape[1]] = rest[:, nq:]
        pg = _dot(a, wg_ref[...]).astype(BF16)
        for wref, bref, off in ((waf_ref, bf_ref, 0), (wab_ref, bb_ref, nq)):
            z = _dot(pg, wref[...]) + bref[...]
            la = (jnp.minimum(z, 0.0) - jnp.log(1.0 + jnp.exp(-jnp.abs(z)))) * (1.0 / GATE_TAU)
            gl_ref[0, :, rest.shape[1] + off:rest.shape[1] + off + nq] = la


def _inproj(xs, mx, mc, g, w, gain, cos_t, sin_t, seg, n_ctx_tiles, n_norm, gla_ops=None):
    b, l, d = xs.shape
    tm = TOK_TILE
    n = w.shape[1]
    gla = gla_ops is not None
    n_lowp = n - (2 * B_HEADS * B_DK + 2 * B_HEADS * B_DV if gla else 0)
    full = lambda shape: pl.BlockSpec(shape, lambda i, j: (0,) * len(shape))
    in_specs = [
        pl.BlockSpec((1, tm, d), lambda i, j: (i, j, 0)),
        pl.BlockSpec((1, 8, d), lambda i, j: (i, 0, 0)),
        full((8, d)), full((1, d)), full((d, n)), full((1, n_norm)),
        pl.BlockSpec((tm, LANES), lambda i, j: (j, 0)),
        pl.BlockSpec((tm, LANES), lambda i, j: (j, 0)),
        full((512, 512)),
    ]
    ops = [xs, mx, mc, g, w, gain, cos_t, sin_t, seg]
    out_shape = [jax.ShapeDtypeStruct((b, l, n_lowp), BF16)]
    out_specs = [pl.BlockSpec((1, tm, n_lowp), lambda i, j: (i, j, 0))]
    if gla:
        wg, waf, wab, bf, bb = gla_ops
        in_specs += [full(wg.shape), full(waf.shape), full(wab.shape), full(bf.shape), full(bb.shape)]
        ops += [wg, waf, wab, bf, bb]
        n_gl = (n - n_lowp) + 2 * B_HEADS * B_DK
        out_shape.append(jax.ShapeDtypeStruct((b, l, n_gl), F32))
        out_specs.append(pl.BlockSpec((1, tm, n_gl), lambda i, j: (i, j, 0)))
    return pl.pallas_call(
        functools.partial(_inproj_kernel, n_ctx_tiles=n_ctx_tiles, n_norm=n_norm, n_lowp=n_lowp, gla=gla),
        grid=(b, l // tm),
        in_specs=in_specs,
        out_specs=out_specs,
        out_shape=out_shape,
        name="inproj_even" if gla else "inproj_odd",
        compiler_params=pltpu.CompilerParams(
            dimension_semantics=("parallel", "arbitrary"), vmem_limit_bytes=VMEM_LIMIT),
    )(*ops)


def _softmax_parts(s):
    m = jnp.max(s, axis=-1, keepdims=True)
    e = jnp.exp2(s - m)
    return e, 1.0 / jnp.sum(e, axis=-1, keepdims=True)


def _diff_body(q, k, v, lam, subln, post_scale):
    lane = lax.broadcasted_iota(jnp.int32, q.shape, 1)
    lo = lane < HEAD_DIM
    zero = jnp.zeros_like(q)
    e0, r0 = _softmax_parts(_dot_nt(jnp.where(lo, q, zero), k))
    e1, r1 = _softmax_parts(_dot_nt(jnp.where(lo, zero, q), k))
    w = e0 * r0 - e1 * (r1 * lam)
    o = _dot(w.astype(BF16), v)
    ms = jnp.mean(o * o, axis=-1, keepdims=True)
    return o * lax.rsqrt(ms + EPS) * subln * post_scale


def _diff_attn_kernel(q_ref, k_ref, v_ref, lam_ref, subln_ref, o_ref, *, n_ctx_tiles, ctx_len, lam_init):
    lv = lam_ref[...]
    lam = (jnp.exp(jnp.sum(lv[0:1] * lv[1:2], axis=-1, keepdims=True))
           - jnp.exp(jnp.sum(lv[2:3] * lv[3:4], axis=-1, keepdims=True)) + lam_init)
    is_ctx = pl.program_id(2) < n_ctx_tiles

    @pl.when(is_ctx)
    def _():
        o_ref[0] = _diff_body(q_ref[0], k_ref[0, :ctx_len], v_ref[0, :ctx_len], lam, subln_ref[...],
                              1.0 - lam_init).astype(o_ref.dtype)

    @pl.when(jnp.logical_not(is_ctx))
    def _():
        o_ref[0] = _diff_body(q_ref[0], k_ref[0], v_ref[0], lam, subln_ref[...],
                              1.0 - lam_init).astype(o_ref.dtype)


def _diff_attn(qkv, lam_vec, subln, n_ctx_tiles, ctx_len, lam_init):
    b, l, _ = qkv.shape
    tq = TOK_TILE
    w = 2 * HEAD_DIM
    return pl.pallas_call(
        functools.partial(_diff_attn_kernel, n_ctx_tiles=n_ctx_tiles, ctx_len=ctx_len, lam_init=lam_init),
        grid=(b, A_HEADS, l // tq),
        in_specs=[
            pl.BlockSpec((1, tq, w), lambda i, h, j: (i, j, h)),
            pl.BlockSpec((1, l, w), lambda i, h, j: (i, 0, A_HEADS + h)),
            pl.BlockSpec((1, l, w), lambda i, h, j: (i, 0, 2 * A_HEADS + h)),
            pl.BlockSpec((4, HEAD_DIM), lambda i, h, j: (0, 0)),
            pl.BlockSpec((1, w), lambda i, h, j: (0, 0)),
        ],
        out_specs=pl.BlockSpec((1, tq, w), lambda i, h, j: (i, j, h)),
        out_shape=jax.ShapeDtypeStruct((b, l, A_HEADS * w), BF16),
        name="diff_attn",
        compiler_params=pltpu.CompilerParams(
            dimension_semantics=("parallel", "parallel", "arbitrary"), vmem_limit_bytes=VMEM_LIMIT),
    )(qkv, qkv, qkv, lam_vec, subln)


def _gqa_body(q, k, v, o_ref):
    group = C_Q_HEADS // C_KV_HEADS
    for kv in range(C_KV_HEADS):
        kd = k[:, kv * LANES:(kv + 1) * LANES]
        vd = v[:, kv * LANES:(kv + 1) * LANES]
        for a in range(group // 2):
            c0 = (kv * group + 2 * a) * HEAD_DIM
            qp = q[:, c0:c0 + LANES]
            lane = lax.broadcasted_iota(jnp.int32, qp.shape, 1)
            lo = lane < HEAD_DIM
            zero = jnp.zeros_like(qp)
            e0, r0 = _softmax_parts(_dot_nt(jnp.where(lo, qp, zero), kd))
            e1, r1 = _softmax_parts(_dot_nt(jnp.where(lo, zero, qp), kd))
            o0 = _dot(e0.astype(BF16), vd) * r0
            o1 = _dot(e1.astype(BF16), vd) * r1
            o_ref[0, :, c0:c0 + LANES] = jnp.where(lo, o0, o1).astype(o_ref.dtype)


def _gqa_attn_kernel(q_ref, k_ref, v_ref, o_ref, *, n_ctx_tiles, ctx_len):
    is_ctx = pl.program_id(1) < n_ctx_tiles

    @pl.when(is_ctx)
    def _():
        _gqa_body(q_ref[0], k_ref[0, :ctx_len], v_ref[0, :ctx_len], o_ref)

    @pl.when(jnp.logical_not(is_ctx))
    def _():
        _gqa_body(q_ref[0], k_ref[0], v_ref[0], o_ref)


def _gqa_attn(qkv, n_ctx_tiles, ctx_len):
    b, l, _ = qkv.shape
    tq = TOK_TILE
    wq = C_Q_HEADS * HEAD_DIM
    wk = 2 * C_KV_HEADS * HEAD_DIM
    return pl.pallas_call(
        functools.partial(_gqa_attn_kernel, n_ctx_tiles=n_ctx_tiles, ctx_len=ctx_len),
        grid=(b, l // tq),
        in_specs=[
            pl.BlockSpec((1, tq, wq), lambda i, j: (i, j, 0)),
            pl.BlockSpec((1, l, wk), lambda i, j: (i, 0, wq // wk)),
            pl.BlockSpec((1, l, wk), lambda i, j: (i, 0, wq // wk + 1)),
        ],
        out_specs=pl.BlockSpec((1, tq, wq), lambda i, j: (i, j, 0)),
        out_shape=jax.ShapeDtypeStruct((b, l, wq), BF16),
        name="gqa_attn",
        compiler_params=pltpu.CompilerParams(
            dimension_semantics=("parallel", "arbitrary"), vmem_limit_bytes=VMEM_LIMIT),
    )(qkv, qkv, qkv)


def _gla_tile(q, k, v, la, st_ref, reverse):
    t = q.shape[0]
    row = lax.broadcasted_iota(jnp.int32, (t, t), 0)
    col = lax.broadcasted_iota(jnp.int32, (t, t), 1)
    same = jnp.where((row // GLA_CHUNK) == (col // GLA_CHUNK), 1.0, 0.0)
    tri = jnp.where((col >= row) if reverse else (col <= row), same, 0.0)
    cum = _split_dot(tri.astype(BF16), la, 3)
    tot = _split_dot(same.astype(BF16), la, 3)
    q_dec = q * jnp.exp(cum)
    k_inv = (k * jnp.exp(-cum)).astype(BF16)
    k_end = (k * jnp.exp(tot - cum)).astype(BF16)
    lane = lax.broadcasted_iota(jnp.int32, q.shape, 1)
    lo = lane < B_DK
    qz = jnp.zeros_like(q_dec)
    vb = v.astype(BF16)
    keep = tri > 0.5
    a_lo = jnp.where(keep, _dot_nt(jnp.where(lo, q_dec, qz).astype(BF16), k_inv), 0.0).astype(BF16)
    a_hi = jnp.where(keep, _dot_nt(jnp.where(lo, qz, q_dec).astype(BF16), k_inv), 0.0).astype(BF16)
    o_intra = jnp.concatenate([_dot(a_lo, vb[:, :B_DV]), _dot(a_hi, vb[:, B_DV:])], axis=1)
    qd = q_dec.astype(BF16)
    srow = lax.broadcasted_iota(jnp.int32, (2 * B_DV, 2 * B_DK), 0)
    scol = lax.broadcasted_iota(jnp.int32, (2 * B_DV, 2 * B_DK), 1)
    own = (srow < B_DV) == (scol < B_DK)
    n_chunks = t // GLA_CHUNK
    outs = [None] * n_chunks
    for c in (range(n_chunks - 1, -1, -1) if reverse else range(n_chunks)):
        r0 = c * GLA_CHUNK
        st = st_ref[...]
        outs[c] = o_intra[r0:r0 + GLA_CHUNK] + _dot_nt(qd[r0:r0 + GLA_CHUNK], st.astype(BF16))
        decay = jnp.exp(tot[r0:r0 + 1, :])
        d_state = _dot(v[r0:r0 + GLA_CHUNK].T.astype(BF16), k_end[r0:r0 + GLA_CHUNK])
        st_ref[...] = st * decay + jnp.where(own, d_state, 0.0)
    return jnp.concatenate(outs, axis=0)


def _gla_kernel(qf, kf, vf, lf, qb, kb, vb, lb, of_ref, ob_ref, stf_ref, stb_ref):
    @pl.when(pl.program_id(2) == 0)
    def _():
        stf_ref[...] = jnp.zeros_like(stf_ref)
        stb_ref[...] = jnp.zeros_like(stb_ref)

    of_ref[0] = _gla_tile(qf[0], kf[0], vf[0], lf[0], stf_ref, False)
    ob_ref[0] = _gla_tile(qb[0], kb[0], vb[0], lb[0], stb_ref, True)


def _gla(gl, n_ctx_tiles):
    b, l, _ = gl.shape
    t = TOK_TILE
    nt = l // t
    pairs = B_HEADS // 2
    wk = 2 * B_DK
    wv = 2 * B_DV
    nq = B_HEADS * B_DK
    la_f0 = (2 * nq + 2 * B_HEADS * B_DV) // wk

    def bwd(j):
        return jnp.where(j < n_ctx_tiles, n_ctx_tiles - 1 - j, nt - 1 - (j - n_ctx_tiles))

    def specs(tile):
        return [
            pl.BlockSpec((1, t, wk), lambda i, p, j: (i, tile(j), p)),
            pl.BlockSpec((1, t, wk), lambda i, p, j: (i, tile(j), nq // wk + p)),
            pl.BlockSpec((1, t, wv), lambda i, p, j: (i, tile(j), 2 * nq // wv + p)),
        ]

    fwd = lambda j: j
    in_specs = (specs(fwd) + [pl.BlockSpec((1, t, wk), lambda i, p, j: (i, j, la_f0 + p))]
                + specs(bwd) + [pl.BlockSpec((1, t, wk), lambda i, p, j: (i, bwd(j), la_f0 + nq // wk + p))])
    return pl.pallas_call(
        _gla_kernel,
        grid=(b, pairs, nt),
        in_specs=in_specs,
        out_specs=[pl.BlockSpec((1, t, wv), lambda i, p, j: (i, j, p)),
                   pl.BlockSpec((1, t, wv), lambda i, p, j: (i, bwd(j), p))],
        out_shape=[jax.ShapeDtypeStruct((b, l, B_HEADS * B_DV), F32)] * 2,
        scratch_shapes=[pltpu.VMEM((wv, wk), F32), pltpu.VMEM((wv, wk), F32)],
        name="gla",
        compiler_params=pltpu.CompilerParams(
            dimension_semantics=("parallel", "parallel", "arbitrary"), vmem_limit_bytes=VMEM_LIMIT),
    )(*([gl] * 8))


def _outproj_even_kernel(x_ref, mx_ref, mc_ref, d_ref, of_ref, ob_ref, r_ref, gain_ref, w_ref, o_ref,
                         *, n_ctx_tiles):
    is_ctx = pl.program_id(1) < n_ctx_tiles
    gate = _pick_mod(mx_ref, mc_ref, 2, is_ctx)
    o = of_ref[0] + ob_ref[0]
    r = r_ref[0]
    parts = []
    for c in range(0, o.shape[1], B_DV):
        oc = o[:, c:c + B_DV]
        ms = jnp.mean(oc * oc, axis=-1, keepdims=True)
        parts.append(oc * lax.rsqrt(ms + EPS))
    g = jnp.concatenate(parts, axis=1) * gain_ref[...] * (r / (1.0 + jnp.exp(-r)))
    nd = d_ref.shape[2]
    y = _dot(d_ref[0], w_ref[:nd, :]) + _dot(g.astype(BF16), w_ref[nd:, :])
    o_ref[0] = x_ref[0] + gate * y


def _outproj_odd_kernel(x_ref, mx_ref, mc_ref, a_ref, w_ref, o_ref, *, n_ctx_tiles):
    is_ctx = pl.program_id(1) < n_ctx_tiles
    gate = _pick_mod(mx_ref, mc_ref, 2, is_ctx)
    o_ref[0] = x_ref[0] + gate * _dot(a_ref[0], w_ref[...])


def _outproj(xs, mx, mc, w, n_ctx_tiles, attn, gla_ops=None):
    b, l, d = xs.shape
    tm = TOK_TILE
    tok = lambda width, blk=0: pl.BlockSpec((1, tm, width), lambda i, j: (i, j, blk))
    full = lambda shape: pl.BlockSpec(shape, lambda i, j: (0,) * len(shape))
    head = [tok(d), pl.BlockSpec((1, 8, d), lambda i, j: (i, 0, 0)), full((8, d))]
    if gla_ops is not None:
        o_f, o_b, gl, gain = gla_ops
        wv = B_HEADS * B_DV
        r_blk = (2 * B_HEADS * B_DK + wv) // wv
        kern = functools.partial(_outproj_even_kernel, n_ctx_tiles=n_ctx_tiles)
        in_specs = head + [tok(attn.shape[2]), tok(wv), tok(wv), tok(wv, r_blk), full((1, wv)), full(w.shape)]
        ops = [xs, mx, mc, attn, o_f, o_b, gl, gain, w]
    else:
        kern = functools.partial(_outproj_odd_kernel, n_ctx_tiles=n_ctx_tiles)
        in_specs = head + [tok(attn.shape[2]), full(w.shape)]
        ops = [xs, mx, mc, attn, w]
    return pl.pallas_call(
        kern,
        grid=(b, l // tm),
        in_specs=in_specs,
        out_specs=tok(d),
        out_shape=jax.ShapeDtypeStruct((b, l, d), F32),
        name="outproj_even" if gla_ops is not None else "outproj_odd",
        compiler_params=pltpu.CompilerParams(
            dimension_semantics=("parallel", "arbitrary"), vmem_limit_bytes=VMEM_LIMIT),
    )(*ops)


def _ce(v, i, j):
    a, b = v[i], v[j]
    v[i] = jnp.maximum(a, b)
    v[j] = jnp.minimum(a, b)


def _bitonic_sort_desc(v):
    n = len(v)
    k = 2
    while k <= n:
        j = k // 2
        while j >= 1:
            for i in range(n):
                m = i ^ j
                if m > i:
                    if (i & k) == 0:
                        _ce(v, i, m)
                    else:
                        _ce(v, m, i)
            j //= 2
        k *= 2


def _bitonic_merge_desc(v):
    n = len(v)
    j = n // 2
    while j >= 1:
        for i in range(n):
            m = i ^ j
            if m > i:
                _ce(v, i, m)
        j //= 2


def _merge_top(a, b):
    n = len(a)
    c = [jnp.maximum(a[i], b[n - 1 - i]) for i in range(n)]
    _bitonic_merge_desc(c)
    return c


def _top_sorted(s):
    groups = s.shape[0] // SUBLANES
    v = [s[g * SUBLANES:(g + 1) * SUBLANES, :] for g in range(groups)]
    _bitonic_sort_desc(v)
    v = v[:PEER_TOPK]
    shift = SUBLANES // 2
    while shift >= 1:
        v = _merge_top(v, [pltpu.roll(x, shift, axis=0) for x in v])
        shift //= 2
    return v


def _gelu_tanh(x):
    k = -2.0 * 0.7978845608028654 * math.log2(math.e)
    t = x * (k + (k * 0.044715) * (x * x))
    return x * (1.0 / (1.0 + jnp.exp2(t)))


def _routing_weights(e1_ref, tau_ref, e2_ref, s2_ref, i0, n_rows):
    tm = e1_ref.shape[2]
    shape = (PEER_NKEYS // SUBLANES, SUBLANES, LANES)
    taus = [[jnp.broadcast_to(tau_ref[hd, pl.ds(i0 + r, 1), :], (SUBLANES, tm))
             for hd in range(PEER_HEADS)] for r in range(n_rows)]
    e1s = [[jnp.broadcast_to(e1_ref[hd, pl.ds(i0 + r, 1), :], (SUBLANES, tm))
            for hd in range(PEER_HEADS)] for r in range(n_rows)]
    cols = [[] for _ in range(n_rows)]
    for c0 in range(0, tm, LANES):
        g = [None] * n_rows
        for hd in range(PEER_HEADS):
            s2 = s2_ref[hd, :, c0:c0 + LANES].reshape(shape)
            e2 = e2_ref[hd, :, c0:c0 + LANES].reshape(shape)
            zero = jnp.zeros_like(e2)
            for r in range(n_rows):
                tau = taus[r][hd][:, c0:c0 + LANES]
                e1 = e1s[r][hd][:, c0:c0 + LANES]
                p = jnp.where(s2 >= tau[None], e1[None] * e2, zero)
                g[r] = p if g[r] is None else g[r] + p
        for r in range(n_rows):
            cols[r].append(g[r].reshape(PEER_NKEYS, LANES))
    return [jnp.concatenate(c, axis=1) for c in cols]


def _peer_kernel(x_ref, mx_ref, mc_ref, g_ref, wqt_ref, keys_ref, u_ref, vt_ref, o_ref,
                 ht_ref, qt_ref, acc_ref, e1_ref, tau_ref, e2_ref, s2_ref,
                 *, ctx_len, rows_i, n_blocks):
    tm = x_ref.shape[1]
    step = pl.program_id(2)
    row = pl.program_id(1) * tm + lax.broadcasted_iota(jnp.int32, (tm, 1), 0)
    is_ctx = row < ctx_len
    half = PEER_DK // 2

    def scores(hd):
        s1 = _dot(keys_ref[hd, 0], qt_ref[hd * PEER_DK:hd * PEER_DK + half, :])
        s2 = _dot(keys_ref[hd, 1], qt_ref[hd * PEER_DK + half:(hd + 1) * PEER_DK, :])
        return s1, s2

    @pl.when(step == 0)
    def _():
        shift = jnp.where(is_ctx, mc_ref[3:4, :], mx_ref[0, 3:4, :])
        scale = jnp.where(is_ctx, mc_ref[4:5, :], mx_ref[0, 4:5, :])
        h = _rms_mod(x_ref[0], g_ref[...], shift, scale)
        ht = h.T.astype(BF16)
        ht_ref[...] = ht
        acc_ref[...] = jnp.zeros_like(acc_ref)
        qt_ref[...] = _dot(wqt_ref[...], ht).astype(BF16)
        sub = lax.broadcasted_iota(jnp.int32, (SUBLANES, tm), 0)
        a_all = [jnp.zeros((SUBLANES, tm), F32)] * PEER_TOPK
        b_all = [jnp.zeros((SUBLANES, tm), F32)] * PEER_TOPK
        for hd in range(PEER_HEADS):
            s1, s2 = scores(hd)
            ta = _top_sorted(s1)
            tb = _top_sorted(s2)
            a_all = [jnp.where(sub == hd, ta[k], a_all[k]) for k in range(PEER_TOPK)]
            b_all = [jnp.where(sub == hd, tb[k], b_all[k]) for k in range(PEER_TOPK)]
        neg = jnp.full((SUBLANES, tm), NEG_BIG, F32)
        top = [a_all[0] + b_all[c] for c in range(PEER_TOPK)]
        for r in range(1, PEER_TOPK):
            n_r = PEER_TOPK // (r + 1)
            top = _merge_top(top, [a_all[r] + b_all[c] for c in range(n_r)] + [neg] * (PEER_TOPK - n_r))
        z = jnp.zeros((SUBLANES, tm), F32)
        for k in range(PEER_TOPK):
            z = z + jnp.exp(top[k] - top[0])
        thr = top[PEER_TOPK - 1]
        inv_z = 1.0 / z
        big = jnp.full((SUBLANES, tm), -NEG_BIG, F32)
        theta = []
        for k in range(PEER_TOPK):
            t = big
            for r in range(PEER_TOPK // (k + 1)):
                t = jnp.minimum(t, jnp.where(a_all[r] + b_all[k] >= thr, a_all[r], big))
            theta.append(t)
        for hd in range(PEER_HEADS):
            s1, s2 = scores(hd)
            tau = jnp.full(s1.shape, -NEG_BIG, F32)
            for k in range(PEER_TOPK):
                tau = jnp.where(s1 >= theta[k][hd:hd + 1, :], b_all[k][hd:hd + 1, :], tau)
            tau_ref[hd] = tau
            s2_ref[hd] = s2
            e1_ref[hd] = jnp.exp(s1 - a_all[0][hd:hd + 1, :]) * inv_z[hd:hd + 1, :]
            e2_ref[hd] = jnp.exp(s2 - b_all[0][hd:hd + 1, :])

    gs = []
    for i0 in range(0, rows_i, PEER_ROW_GROUP):
        gs += _routing_weights(e1_ref, tau_ref, e2_ref, s2_ref, step * rows_i + i0, PEER_ROW_GROUP)
    at = _dot(u_ref[...], ht_ref[...])
    wt = [(_gelu_tanh(at[ii * PEER_NKEYS:(ii + 1) * PEER_NKEYS, :]) * gs[ii]).astype(BF16)
          for ii in range(rows_i)]
    acc_ref[...] += _dot(vt_ref[0], jnp.concatenate(wt, axis=0))

    @pl.when(step == n_blocks - 1)
    def _():
        gate = jnp.where(is_ctx, mc_ref[5:6, :], mx_ref[0, 5:6, :])
        o_ref[0] = x_ref[0] + gate * acc_ref[...].T


def _peer(xs, mx, mc, g, wqt, keys, u, vt, ctx_len, tm):
    b, l, d = xs.shape
    eblk = PEER_EXPERT_BLOCK
    n_blocks = u.shape[0] // eblk
    full = lambda shape: pl.BlockSpec(shape, lambda i, j, e: (0,) * len(shape))
    sc = (PEER_HEADS, PEER_NKEYS, tm)
    return pl.pallas_call(
        functools.partial(_peer_kernel, ctx_len=ctx_len, rows_i=eblk // PEER_NKEYS, n_blocks=n_blocks),
        grid=(b, l // tm, n_blocks),
        in_specs=[
            pl.BlockSpec((1, tm, d), lambda i, j, e: (i, j, 0)),
            pl.BlockSpec((1, 8, d), lambda i, j, e: (i, 0, 0)),
            full((8, d)), full((1, d)),
            pl.BlockSpec(wqt.shape, lambda i, j, e: (0, 0), pipeline_mode=pl.Buffered(1)),
            full(keys.shape),
            pl.BlockSpec((eblk, d), lambda i, j, e: (e, 0)),
            pl.BlockSpec((1, d, eblk), lambda i, j, e: (e, 0, 0)),
        ],
        out_specs=pl.BlockSpec((1, tm, d), lambda i, j, e: (i, j, 0)),
        out_shape=jax.ShapeDtypeStruct((b, l, d), F32),
        scratch_shapes=[
            pltpu.VMEM((d, tm), BF16), pltpu.VMEM((PEER_HEADS * PEER_DK, tm), BF16),
            pltpu.VMEM((d, tm), F32),
            pltpu.VMEM(sc, F32), pltpu.VMEM(sc, F32), pltpu.VMEM(sc, F32), pltpu.VMEM(sc, F32),
        ],
        name="peer",
        compiler_params=pltpu.CompilerParams(
            dimension_semantics=("parallel", "parallel", "arbitrary"), vmem_limit_bytes=VMEM_LIMIT),
    )(xs, mx, mc, g, wqt, keys, u, vt)


def kernel(x, c, ctx, c_ctx, w_mod, b_mod, g_mix, g_ffn, w_in_even, w_out_even, a_q_gain, a_k_gain,
           a_lambda, a_subln, b_w_af, b_b_af, b_w_ab, b_b_ab, b_gain, w_in_odd, w_out_odd, c_q_gain,
           c_k_gain, peer_wq, peer_keys, peer_u, peer_v):
    bsz, seq, d = x.shape
    n_ctx = ctx.shape[1]
    depth = w_mod.shape[0]
    n_ctx_tiles = n_ctx // TOK_TILE
    assert n_ctx % TOK_TILE == 0 and seq % TOK_TILE == 0 and (n_ctx + seq) % PEER_TOK_TILE == 0

    rows = seq // GRID_W
    row = jnp.repeat(jnp.arange(rows, dtype=F32), GRID_W)
    col = jnp.tile(jnp.arange(GRID_W, dtype=F32), rows)
    inv_freq = ROPE_THETA ** (-jnp.arange(ROPE_FREQS, dtype=F32) / ROPE_FREQS)
    ang = jnp.stack([row, col], axis=-1)[:, :, None] * inv_freq
    cos = jnp.cos(ang)
    sin = jnp.sin(ang)
    cos64 = jnp.concatenate([cos[:, 0], cos[:, 0], cos[:, 1], cos[:, 1]], axis=-1)
    sin64 = jnp.concatenate([-sin[:, 0], sin[:, 0], -sin[:, 1], sin[:, 1]], axis=-1)
    cos_t = jnp.concatenate([jnp.ones((n_ctx, LANES), F32), jnp.tile(cos64, (1, 2))], axis=0)
    sin_t = jnp.concatenate([jnp.zeros((n_ctx, LANES), F32), jnp.tile(sin64, (1, 2))], axis=0)
    seg = jnp.asarray(np.kron(np.eye(512 // HEAD_DIM), np.ones((HEAD_DIM, HEAD_DIM))), BF16)

    pad = (-(bsz + 1)) % SUBLANES
    cc = jnp.concatenate([c, c_ctx[None, :], jnp.zeros((pad, d), F32)], axis=0)
    mod = _modulation(cc, w_mod, b_mod)
    mod = mod.reshape(depth, bsz + 1 + pad, 6, d)
    mod = jnp.concatenate([mod, jnp.zeros((depth, bsz + 1 + pad, 2, d), F32)], axis=2)

    xs = jnp.concatenate([ctx, x], axis=1)
    scale = HEAD_DIM ** -0.5 * math.log2(math.e)
    for layer in range(depth):
        mx = mod[layer, :bsz]
        mc = mod[layer, bsz]
        i = layer // 2
        g1 = g_mix[layer][None, :]
        if layer % 2 == 0:
            lam_init = 0.8 - 0.6 * math.exp(-0.3 * layer)
            w = w_in_even[i]
            n_main = w.shape[1] - 2 * GATE_RANK
            gain = jnp.concatenate([jnp.tile(a_q_gain[i] * scale, 2 * A_HEADS),
                                    jnp.tile(a_k_gain[i], 2 * A_HEADS)])[None, :]
            wg = jnp.pad(w[:, n_main:], ((0, 0), (0, LANES - 2 * GATE_RANK))).astype(BF16)
            waf = jnp.pad(b_w_af[i], ((0, LANES - GATE_RANK), (0, 0))).astype(BF16)
            wab = jnp.pad(b_w_ab[i], ((GATE_RANK, LANES - 2 * GATE_RANK), (0, 0))).astype(BF16)
            qkv, gl = _inproj(xs, mx, mc, g1, w[:, :n_main].astype(BF16), gain, cos_t, sin_t, seg,
                              n_ctx_tiles, gain.shape[1],
                              gla_ops=(wg, waf, wab, b_b_af[i][None, :], b_b_ab[i][None, :]))
            dx = _diff_attn(qkv, a_lambda[i], a_subln[i][None, :], n_ctx_tiles, n_ctx, lam_init)
            o_f, o_b = _gla(gl, n_ctx_tiles)
            xs = _outproj(xs, mx, mc, w_out_even[i].astype(BF16), n_ctx_tiles, dx,
                          gla_ops=(o_f, o_b, gl, jnp.tile(b_gain[i], B_HEADS)[None, :]))
        else:
            w = w_in_odd[i]
            nq = C_Q_HEADS * HEAD_DIM
            nk = C_KV_HEADS * HEAD_DIM
            dup = lambda m: jnp.repeat(m.reshape(d, C_KV_HEADS, 1, HEAD_DIM), 2, axis=2).reshape(d, 2 * nk)
            w2 = jnp.concatenate([w[:, :nq], dup(w[:, nq:nq + nk]), dup(w[:, nq + nk:])], axis=1)
            gain = jnp.concatenate([jnp.tile(c_q_gain[i] * scale, C_Q_HEADS),
                                    jnp.tile(c_k_gain[i], 2 * C_KV_HEADS)])[None, :]
            qkv = _inproj(xs, mx, mc, g1, w2.astype(BF16), gain, cos_t, sin_t, seg,
                          n_ctx_tiles, gain.shape[1])[0]
            att = _gqa_attn(qkv, n_ctx_tiles, n_ctx)
            xs = _outproj(xs, mx, mc, w_out_odd[i].astype(BF16), n_ctx_tiles, att)
        v_blocks = peer_v[layer].reshape(-1, PEER_EXPERT_BLOCK, d).transpose(0, 2, 1).astype(BF16)
        xs = _peer(xs, mx, mc, g_ffn[layer][None, :], peer_wq[layer].T.astype(BF16),
                   peer_keys[layer].astype(BF16), peer_u[layer].astype(BF16),
                   v_blocks, n_ctx, PEER_TOK_TILE)
    return xs[:, n_ctx:, :]
```

```python
import functools
import math

import numpy as np
import jax
import jax.numpy as jnp
from jax import lax
from jax.experimental import pallas as pl
from jax.experimental.pallas import tpu as pltpu

F32 = jnp.float32
BF16 = jnp.bfloat16

EPS = 1e-6
GRID_W = 64
HEAD_DIM = 64
ROPE_FREQS = HEAD_DIM // 4
ROPE_THETA = 10000.0
A_HEADS = 4
B_HEADS = 4
B_DK = 64
B_DV = 128
GATE_RANK = 16
GATE_TAU = 16.0
GLA_CHUNK = 64
C_Q_HEADS = 16
C_KV_HEADS = 4
PEER_HEADS = 8
PEER_NKEYS = 128
PEER_DK = 128
PEER_TOPK = 16

LANES = 128
SUBLANES = 8
VMEM_LIMIT = 56 * 1024 * 1024
NEG_BIG = -3.0e38

TOK_TILE = 256
PEER_TOK_TILE = 768
PEER_EXPERT_BLOCK = 1024
PEER_ROW_GROUP = 2


def _dot(a, b):
    return jnp.dot(a, b, preferred_element_type=F32)


def _dot_nt(a, b):
    return lax.dot_general(a, b, (((1,), (1,)), ((), ())), preferred_element_type=F32)


def _split_dot(onehot_bf16, x, parts):
    acc = None
    rem = x
    for _ in range(parts):
        piece = rem.astype(BF16)
        rem = rem - piece.astype(F32)
        t = _dot(onehot_bf16, piece)
        acc = t if acc is None else acc + t
    return acc


def _split_dot_right(x, onehot_bf16, parts):
    acc = None
    rem = x
    for _ in range(parts):
        piece = rem.astype(BF16)
        rem = rem - piece.astype(F32)
        t = _dot(piece, onehot_bf16)
        acc = t if acc is None else acc + t
    return acc


def _rms_mod(x, g, shift, scale):
    ms = jnp.mean(x * x, axis=-1, keepdims=True)
    y = x * lax.rsqrt(ms + EPS) * g
    return y * (1.0 + scale) + shift


def _pick_mod(mx_ref, mc_ref, k, is_ctx):
    return jnp.where(is_ctx, mc_ref[k:k + 1, :], mx_ref[0, k:k + 1, :])


def _mod_kernel(c_ref, w_ref, b_ref, o_ref):
    c = c_ref[...]
    s = c / (1.0 + jnp.exp(-c))
    o_ref[0] = _dot(s, w_ref[0]) + b_ref[0]


def _modulation(cc, w_mod, b_mod):
    depth, d, n = w_mod.shape
    rows = cc.shape[0]
    tn = d
    return pl.pallas_call(
        _mod_kernel,
        grid=(depth, n // tn),
        in_specs=[
            pl.BlockSpec((rows, d), lambda l, j: (0, 0)),
            pl.BlockSpec((1, d, tn), lambda l, j: (l, 0, j)),
            pl.BlockSpec((1, 1, tn), lambda l, j: (l, 0, j)),
        ],
        out_specs=pl.BlockSpec((1, rows, tn), lambda l, j: (l, 0, j)),
        out_shape=jax.ShapeDtypeStruct((depth, rows, n), F32),
        name="adaln_mod",
        compiler_params=pltpu.CompilerParams(
            dimension_semantics=("arbitrary", "arbitrary"), vmem_limit_bytes=VMEM_LIMIT),
    )(cc, w_mod, b_mod.reshape(depth, 1, n))


def _head_norm_rope(t, seg, gain, cos, sin_signed, first_half):
    outs = []
    for c0 in range(0, t.shape[1], 512):
        tc = t[:, c0:c0 + 512]
        ss = _split_dot_right(tc * tc, seg, 2)
        tn = tc * lax.rsqrt(ss * (1.0 / HEAD_DIM) + EPS) * gain[:, c0:c0 + 512]
        for c in range(0, 512, LANES):
            b = tn[:, c:c + LANES]
            fwd = pltpu.roll(b, LANES - ROPE_FREQS, axis=1)
            bwd = pltpu.roll(b, ROPE_FREQS, axis=1)
            partner = jnp.where(first_half, fwd, bwd)
            outs.append(b * cos + partner * sin_signed)
    return jnp.concatenate(outs, axis=1)


def _inproj_kernel(*refs, n_ctx_tiles, n_norm, n_lowp, gla):
    if gla:
        (x_ref, mx_ref, mc_ref, g_ref, w_ref, gain_ref, cos_ref, sin_ref, seg_ref,
         wg_ref, waf_ref, wab_ref, bf_ref, bb_ref, qkv_ref, gl_ref) = refs
    else:
        (x_ref, mx_ref, mc_ref, g_ref, w_ref, gain_ref, cos_ref, sin_ref, seg_ref,
         qkv_ref) = refs
    is_ctx = pl.program_id(1) < n_ctx_tiles
    shift = _pick_mod(mx_ref, mc_ref, 0, is_ctx)
    scale = _pick_mod(mx_ref, mc_ref, 1, is_ctx)
    a = _rms_mod(x_ref[0], g_ref[...], shift, scale).astype(BF16)
    p = _dot(a, w_ref[...])
    lane = lax.broadcasted_iota(jnp.int32, (1, LANES), 1)
    first_half = (lane % (2 * ROPE_FREQS)) < ROPE_FREQS
    qk = _head_norm_rope(p[:, :n_norm], seg_ref[...], gain_ref[...], cos_ref[...], sin_ref[...],
                         first_half)
    qkv_ref[0, :, :n_norm] = qk.astype(BF16)
    qkv_ref[0, :, n_norm:] = p[:, n_norm:n_lowp].astype(BF16)
    if gla:
        nq = B_HEADS * B_DK
        rest = p[:, n_lowp:]
        gl_ref[0, :, :nq] = rest[:, :nq] * (B_DK ** -0.5)
        gl_ref[0, :, nq:rest.shape[1]] = rest[:, nq:]
        pg = _dot(a, wg_ref[...]).astype(BF16)
        for wref, bref, off in ((waf_ref, bf_ref, 0), (wab_ref, bb_ref, nq)):
            z = _dot(pg, wref[...]) + bref[...]
            la = (jnp.minimum(z, 0.0) - jnp.log(1.0 + jnp.exp(-jnp.abs(z)))) * (1.0 / GATE_TAU)
            gl_ref[0, :, rest.shape[1] + off:rest.shape[1] + off + nq] = la


def _inproj(xs, mx, mc, g, w, gain, cos_t, sin_t, seg, n_ctx_tiles, n_norm, gla_ops=None):
    b, l, d = xs.shape
    tm = TOK_TILE
    n = w.shape[1]
    gla = gla_ops is not None
    n_lowp = n - (2 * B_HEADS * B_DK + 2 * B_HEADS * B_DV if gla else 0)
    full = lambda shape: pl.BlockSpec(shape, lambda i, j: (0,) * len(shape))
    in_specs = [
        pl.BlockSpec((1, tm, d), lambda i, j: (i, j, 0)),
        pl.BlockSpec((1, 8, d), lambda i, j: (i, 0, 0)),
        full((8, d)), full((1, d)), full((d, n)), full((1, n_norm)),
        pl.BlockSpec((tm, LANES), lambda i, j: (j, 0)),
        pl.BlockSpec((tm, LANES), lambda i, j: (j, 0)),
        full((512, 512)),
    ]
    ops = [xs, mx, mc, g, w, gain, cos_t, sin_t, seg]
    out_shape = [jax.ShapeDtypeStruct((b, l, n_lowp), BF16)]
    out_specs = [pl.BlockSpec((1, tm, n_lowp), lambda i, j: (i, j, 0))]
    if gla:
        wg, waf, wab, bf, bb = gla_ops
        in_specs += [full(wg.shape), full(waf.shape), full(wab.shape), full(bf.shape), full(bb.shape)]
        ops += [wg, waf, wab, bf, bb]
        n_gl = (n - n_lowp) + 2 * B_HEADS * B_DK
        out_shape.append(jax.ShapeDtypeStruct((b, l, n_gl), F32))
        out_specs.append(pl.BlockSpec((1, tm, n_gl), lambda i, j: (i, j, 0)))
    return pl.pallas_call(
        functools.partial(_inproj_kernel, n_ctx_tiles=n_ctx_tiles, n_norm=n_norm, n_lowp=n_lowp, gla=gla),
        grid=(b, l // tm),
        in_specs=in_specs,
        out_specs=out_specs,
        out_shape=out_shape,
        name="inproj_even" if gla else "inproj_odd",
        compiler_params=pltpu.CompilerParams(
            dimension_semantics=("parallel", "arbitrary"), vmem_limit_bytes=VMEM_LIMIT),
    )(*ops)


def _softmax_parts(s):
    m = jnp.max(s, axis=-1, keepdims=True)
    e = jnp.exp2(s - m)
    return e, 1.0 / jnp.sum(e, axis=-1, keepdims=True)


def _diff_body(q, k, v, lam, subln, post_scale):
    lane = lax.broadcasted_iota(jnp.int32, q.shape, 1)
    lo = lane < HEAD_DIM
    zero = jnp.zeros_like(q)
    e0, r0 = _softmax_parts(_dot_nt(jnp.where(lo, q, zero), k))
    e1, r1 = _softmax_parts(_dot_nt(jnp.where(lo, zero, q), k))
    w = e0 * r0 - e1 * (r1 * lam)
    o = _dot(w.astype(BF16), v)
    ms = jnp.mean(o * o, axis=-1, keepdims=True)
    return o * lax.rsqrt(ms + EPS) * subln * post_scale


def _diff_attn_kernel(q_ref, k_ref, v_ref, lam_ref, subln_ref, o_ref, *, n_ctx_tiles, ctx_len, lam_init):
    lv = lam_ref[...]
    lam = (jnp.exp(jnp.sum(lv[0:1] * lv[1:2], axis=-1, keepdims=True))
           - jnp.exp(jnp.sum(lv[2:3] * lv[3:4], axis=-1, keepdims=True)) + lam_init)
    is_ctx = pl.program_id(1) < n_ctx_tiles
    w = 2 * HEAD_DIM

    def heads(lk):
        for h in range(A_HEADS):
            cols = slice(h * w, (h + 1) * w)
            o_ref[0, :, cols] = _diff_body(q_ref[0, :, cols], k_ref[0, :lk, cols], v_ref[0, :lk, cols],
                                           lam, subln_ref[...], 1.0 - lam_init).astype(o_ref.dtype)

    @pl.when(is_ctx)
    def _():
        heads(ctx_len)

    @pl.when(jnp.logical_not(is_ctx))
    def _():
        heads(k_ref.shape[1])


def _diff_attn(qkv, lam_vec, subln, n_ctx_tiles, ctx_len, lam_init):
    b, l, _ = qkv.shape
    tq = TOK_TILE
    w = A_HEADS * 2 * HEAD_DIM
    return pl.pallas_call(
        functools.partial(_diff_attn_kernel, n_ctx_tiles=n_ctx_tiles, ctx_len=ctx_len, lam_init=lam_init),
        grid=(b, l // tq),
        in_specs=[
            pl.BlockSpec((1, tq, w), lambda i, j: (i, j, 0)),
            pl.BlockSpec((1, l, w), lambda i, j: (i, 0, 1)),
            pl.BlockSpec((1, l, w), lambda i, j: (i, 0, 2)),
            pl.BlockSpec((4, HEAD_DIM), lambda i, j: (0, 0)),
            pl.BlockSpec((1, 2 * HEAD_DIM), lambda i, j: (0, 0)),
        ],
        out_specs=pl.BlockSpec((1, tq, w), lambda i, j: (i, j, 0)),
        out_shape=jax.ShapeDtypeStruct((b, l, w), BF16),
        name="diff_attn",
        compiler_params=pltpu.CompilerParams(
            dimension_semantics=("parallel", "arbitrary"), vmem_limit_bytes=VMEM_LIMIT),
    )(qkv, qkv, qkv, lam_vec, subln)


def _gqa_body(q, k, v, o_ref):
    group = C_Q_HEADS // C_KV_HEADS
    for kv in range(C_KV_HEADS):
        kd = k[:, kv * LANES:(kv + 1) * LANES]
        vd = v[:, kv * LANES:(kv + 1) * LANES]
        for a in range(group // 2):
            c0 = (kv * group + 2 * a) * HEAD_DIM
            qp = q[:, c0:c0 + LANES]
            lane = lax.broadcasted_iota(jnp.int32, qp.shape, 1)
            lo = lane < HEAD_DIM
            zero = jnp.zeros_like(qp)
            e0, r0 = _softmax_parts(_dot_nt(jnp.where(lo, qp, zero), kd))
            e1, r1 = _softmax_parts(_dot_nt(jnp.where(lo, zero, qp), kd))
            o0 = _dot(e0.astype(BF16), vd) * r0
            o1 = _dot(e1.astype(BF16), vd) * r1
            o_ref[0, :, c0:c0 + LANES] = jnp.where(lo, o0, o1).astype(o_ref.dtype)


def _gqa_attn_kernel(q_ref, k_ref, v_ref, o_ref, *, n_ctx_tiles, ctx_len):
    is_ctx = pl.program_id(1) < n_ctx_tiles

    @pl.when(is_ctx)
    def _():
        _gqa_body(q_ref[0], k_ref[0, :ctx_len], v_ref[0, :ctx_len], o_ref)

    @pl.when(jnp.logical_not(is_ctx))
    def _():
        _gqa_body(q_ref[0], k_ref[0], v_ref[0], o_ref)


def _gqa_attn(qkv, n_ctx_tiles, ctx_len):
    b, l, _ = qkv.shape
    tq = TOK_TILE
    wq = C_Q_HEADS * HEAD_DIM
    wk = 2 * C_KV_HEADS * HEAD_DIM
    return pl.pallas_call(
        functools.partial(_gqa_attn_kernel, n_ctx_tiles=n_ctx_tiles, ctx_len=ctx_len),
        grid=(b, l // tq),
        in_specs=[
            pl.BlockSpec((1, tq, wq), lambda i, j: (i, j, 0)),
            pl.BlockSpec((1, l, wk), lambda i, j: (i, 0, wq // wk)),
            pl.BlockSpec((1, l, wk), lambda i, j: (i, 0, wq // wk + 1)),
        ],
        out_specs=pl.BlockSpec((1, tq, wq), lambda i, j: (i, j, 0)),
        out_shape=jax.ShapeDtypeStruct((b, l, wq), BF16),
        name="gqa_attn",
        compiler_params=pltpu.CompilerParams(
            dimension_semantics=("parallel", "arbitrary"), vmem_limit_bytes=VMEM_LIMIT),
    )(qkv, qkv, qkv)


def _gla_tile(q, k, v, la, st_ref, reverse):
    t = q.shape[0]
    row = lax.broadcasted_iota(jnp.int32, (t, t), 0)
    col = lax.broadcasted_iota(jnp.int32, (t, t), 1)
    same = jnp.where((row // GLA_CHUNK) == (col // GLA_CHUNK), 1.0, 0.0)
    tri = jnp.where((col >= row) if reverse else (col <= row), same, 0.0)
    cum = _split_dot(tri.astype(BF16), la, 3)
    tot = _split_dot(same.astype(BF16), la, 3)
    q_dec = q * jnp.exp(cum)
    k_inv = (k * jnp.exp(-cum)).astype(BF16)
    k_end = (k * jnp.exp(tot - cum)).astype(BF16)
    lane = lax.broadcasted_iota(jnp.int32, q.shape, 1)
    lo = lane < B_DK
    qz = jnp.zeros_like(q_dec)
    vb = v.astype(BF16)
    keep = tri > 0.5
    a_lo = jnp.where(keep, _dot_nt(jnp.where(lo, q_dec, qz).astype(BF16), k_inv), 0.0).astype(BF16)
    a_hi = jnp.where(keep, _dot_nt(jnp.where(lo, qz, q_dec).astype(BF16), k_inv), 0.0).astype(BF16)
    o_intra = jnp.concatenate([_dot(a_lo, vb[:, :B_DV]), _dot(a_hi, vb[:, B_DV:])], axis=1)
    qd = q_dec.astype(BF16)
    srow = lax.broadcasted_iota(jnp.int32, (2 * B_DV, 2 * B_DK), 0)
    scol = lax.broadcasted_iota(jnp.int32, (2 * B_DV, 2 * B_DK), 1)
    own = (srow < B_DV) == (scol < B_DK)
    n_chunks = t // GLA_CHUNK
    outs = [None] * n_chunks
    for c in (range(n_chunks - 1, -1, -1) if reverse else range(n_chunks)):
        r0 = c * GLA_CHUNK
        st = st_ref[...]
        outs[c] = o_intra[r0:r0 + GLA_CHUNK] + _dot_nt(qd[r0:r0 + GLA_CHUNK], st.astype(BF16))
        decay = jnp.exp(tot[r0:r0 + 1, :])
        d_state = _dot(v[r0:r0 + GLA_CHUNK].T.astype(BF16), k_end[r0:r0 + GLA_CHUNK])
        st_ref[...] = st * decay + jnp.where(own, d_state, 0.0)
    return jnp.concatenate(outs, axis=0)


def _gla_kernel(qf, kf, vf, lf, qb, kb, vb, lb, of_ref, ob_ref, stf_ref, stb_ref):
    @pl.when(pl.program_id(2) == 0)
    def _():
        stf_ref[...] = jnp.zeros_like(stf_ref)
        stb_ref[...] = jnp.zeros_like(stb_ref)

    of_ref[0] = _gla_tile(qf[0], kf[0], vf[0], lf[0], stf_ref, False)
    ob_ref[0] = _gla_tile(qb[0], kb[0], vb[0], lb[0], stb_ref, True)


def _gla(gl, n_ctx_tiles):
    b, l, _ = gl.shape
    t = TOK_TILE
    nt = l // t
    pairs = B_HEADS // 2
    wk = 2 * B_DK
    wv = 2 * B_DV
    nq = B_HEADS * B_DK
    la_f0 = (2 * nq + 2 * B_HEADS * B_DV) // wk

    def bwd(j):
        return jnp.where(j < n_ctx_tiles, n_ctx_tiles - 1 - j, nt - 1 - (j - n_ctx_tiles))

    def specs(tile):
        return [
            pl.BlockSpec((1, t, wk), lambda i, p, j: (i, tile(j), p)),
            pl.BlockSpec((1, t, wk), lambda i, p, j: (i, tile(j), nq // wk + p)),
            pl.BlockSpec((1, t, wv), lambda i, p, j: (i, tile(j), 2 * nq // wv + p)),
        ]

    fwd = lambda j: j
    in_specs = (specs(fwd) + [pl.BlockSpec((1, t, wk), lambda i, p, j: (i, j, la_f0 + p))]
                + specs(bwd) + [pl.BlockSpec((1, t, wk), lambda i, p, j: (i, bwd(j), la_f0 + nq // wk + p))])
    return pl.pallas_call(
        _gla_kernel,
        grid=(b, pairs, nt),
        in_specs=in_specs,
        out_specs=[pl.BlockSpec((1, t, wv), lambda i, p, j: (i, j, p)),
                   pl.BlockSpec((1, t, wv), lambda i, p, j: (i, bwd(j), p))],
        out_shape=[jax.ShapeDtypeStruct((b, l, B_HEADS * B_DV), F32)] * 2,
        scratch_shapes=[pltpu.VMEM((wv, wk), F32), pltpu.VMEM((wv, wk), F32)],
        name="gla",
        compiler_params=pltpu.CompilerParams(
            dimension_semantics=("parallel", "parallel", "arbitrary"), vmem_limit_bytes=VMEM_LIMIT),
    )(*([gl] * 8))


def _outproj_even_kernel(x_ref, mx_ref, mc_ref, d_ref, of_ref, ob_ref, r_ref, gain_ref, w_ref, o_ref,
                         *, n_ctx_tiles):
    is_ctx = pl.program_id(1) < n_ctx_tiles
    gate = _pick_mod(mx_ref, mc_ref, 2, is_ctx)
    o = of_ref[0] + ob_ref[0]
    r = r_ref[0]
    parts = []
    for c in range(0, o.shape[1], B_DV):
        oc = o[:, c:c + B_DV]
        ms = jnp.mean(oc * oc, axis=-1, keepdims=True)
        parts.append(oc * lax.rsqrt(ms + EPS))
    g = jnp.concatenate(parts, axis=1) * gain_ref[...] * (r / (1.0 + jnp.exp(-r)))
    nd = d_ref.shape[2]
    y = _dot(d_ref[0], w_ref[:nd, :]) + _dot(g.astype(BF16), w_ref[nd:, :])
    o_ref[0] = x_ref[0] + gate * y


def _outproj_odd_kernel(x_ref, mx_ref, mc_ref, a_ref, w_ref, o_ref, *, n_ctx_tiles):
    is_ctx = pl.program_id(1) < n_ctx_tiles
    gate = _pick_mod(mx_ref, mc_ref, 2, is_ctx)
    o_ref[0] = x_ref[0] + gate * _dot(a_ref[0], w_ref[...])


def _outproj(xs, mx, mc, w, n_ctx_tiles, attn, gla_ops=None):
    b, l, d = xs.shape
    tm = TOK_TILE
    tok = lambda width, blk=0: pl.BlockSpec((1, tm, width), lambda i, j: (i, j, blk))
    full = lambda shape: pl.BlockSpec(shape, lambda i, j: (0,) * len(shape))
    head = [tok(d), pl.BlockSpec((1, 8, d), lambda i, j: (i, 0, 0)), full((8, d))]
    if gla_ops is not None:
        o_f, o_b, gl, gain = gla_ops
        wv = B_HEADS * B_DV
        r_blk = (2 * B_HEADS * B_DK + wv) // wv
        kern = functools.partial(_outproj_even_kernel, n_ctx_tiles=n_ctx_tiles)
        in_specs = head + [tok(attn.shape[2]), tok(wv), tok(wv), tok(wv, r_blk), full((1, wv)), full(w.shape)]
        ops = [xs, mx, mc, attn, o_f, o_b, gl, gain, w]
    else:
        kern = functools.partial(_outproj_odd_kernel, n_ctx_tiles=n_ctx_tiles)
        in_specs = head + [tok(attn.shape[2]), full(w.shape)]
        ops = [xs, mx, mc, attn, w]
    return pl.pallas_call(
        kern,
        grid=(b, l // tm),
        in_specs=in_specs,
        out_specs=tok(d),
        out_shape=jax.ShapeDtypeStruct((b, l, d), F32),
        name="outproj_even" if gla_ops is not None else "outproj_odd",
        compiler_params=pltpu.CompilerParams(
            dimension_semantics=("parallel", "arbitrary"), vmem_limit_bytes=VMEM_LIMIT),
    )(*ops)


def _ce(v, i, j):
    a, b = v[i], v[j]
    v[i] = jnp.maximum(a, b)
    v[j] = jnp.minimum(a, b)


def _bitonic_sort_desc(v):
    n = len(v)
    k = 2
    while k <= n:
        j = k // 2
        while j >= 1:
            for i in range(n):
                m = i ^ j
                if m > i:
                    if (i & k) == 0:
                        _ce(v, i, m)
                    else:
                        _ce(v, m, i)
            j //= 2
        k *= 2


def _bitonic_merge_desc(v):
    n = len(v)
    j = n // 2
    while j >= 1:
        for i in range(n):
            m = i ^ j
            if m > i:
                _ce(v, i, m)
        j //= 2


def _merge_top(a, b):
    n = len(a)
    c = [jnp.maximum(a[i], b[n - 1 - i]) for i in range(n)]
    _bitonic_merge_desc(c)
    return c


def _top_sorted(s):
    groups = s.shape[0] // SUBLANES
    v = [s[g * SUBLANES:(g + 1) * SUBLANES, :] for g in range(groups)]
    _bitonic_sort_desc(v)
    v = v[:PEER_TOPK]
    shift = SUBLANES // 2
    while shift >= 1:
        v = _merge_top(v, [pltpu.roll(x, shift, axis=0) for x in v])
        shift //= 2
    return v


def _gelu_tanh(x):
    k = -2.0 * 0.7978845608028654 * math.log2(math.e)
    t = x * (k + (k * 0.044715) * (x * x))
    return x * (1.0 / (1.0 + jnp.exp2(t)))


def _routing_weights(e1_ref, tau_ref, e2_ref, s2_ref, i0, n_rows):
    tm = e1_ref.shape[2]
    shape = (PEER_NKEYS // SUBLANES, SUBLANES, LANES)
    taus = [[jnp.broadcast_to(tau_ref[hd, pl.ds(i0 + r, 1), :], (SUBLANES, tm))
             for hd in range(PEER_HEADS)] for r in range(n_rows)]
    e1s = [[jnp.broadcast_to(e1_ref[hd, pl.ds(i0 + r, 1), :], (SUBLANES, tm))
            for hd in range(PEER_HEADS)] for r in range(n_rows)]
    cols = [[] for _ in range(n_rows)]
    for c0 in range(0, tm, LANES):
        g = [None] * n_rows
        for hd in range(PEER_HEADS):
            s2 = s2_ref[hd, :, c0:c0 + LANES].reshape(shape)
            e2 = e2_ref[hd, :, c0:c0 + LANES].reshape(shape)
            zero = jnp.zeros_like(e2)
            for r in range(n_rows):
                tau = taus[r][hd][:, c0:c0 + LANES]
                e1 = e1s[r][hd][:, c0:c0 + LANES]
                p = jnp.where(s2 >= tau[None], e1[None] * e2, zero)
                g[r] = p if g[r] is None else g[r] + p
        for r in range(n_rows):
            cols[r].append(g[r].reshape(PEER_NKEYS, LANES))
    return [jnp.concatenate(c, axis=1) for c in cols]


def _peer_kernel(x_ref, mx_ref, mc_ref, g_ref, wqt_ref, keys_ref, u_ref, vt_ref, o_ref,
                 ht_ref, qt_ref, acc_ref, e1_ref, tau_ref, e2_ref, s2_ref,
                 *, ctx_len, rows_i, n_blocks):
    tm = x_ref.shape[1]
    step = pl.program_id(2)
    row = pl.program_id(1) * tm + lax.broadcasted_iota(jnp.int32, (tm, 1), 0)
    is_ctx = row < ctx_len
    half = PEER_DK // 2

    def scores(hd):
        s1 = _dot(keys_ref[hd, 0], qt_ref[hd * PEER_DK:hd * PEER_DK + half, :])
        s2 = _dot(keys_ref[hd, 1], qt_ref[hd * PEER_DK + half:(hd + 1) * PEER_DK, :])
        return s1, s2

    @pl.when(step == 0)
    def _():
        shift = jnp.where(is_ctx, mc_ref[3:4, :], mx_ref[0, 3:4, :])
        scale = jnp.where(is_ctx, mc_ref[4:5, :], mx_ref[0, 4:5, :])
        h = _rms_mod(x_ref[0], g_ref[...], shift, scale)
        ht = h.T.astype(BF16)
        ht_ref[...] = ht
        acc_ref[...] = jnp.zeros_like(acc_ref)
        qt_ref[...] = _dot(wqt_ref[...], ht).astype(BF16)
        sub = lax.broadcasted_iota(jnp.int32, (SUBLANES, tm), 0)
        a_all = [jnp.zeros((SUBLANES, tm), F32)] * PEER_TOPK
        b_all = [jnp.zeros((SUBLANES, tm), F32)] * PEER_TOPK
        for hd in range(PEER_HEADS):
            s1, s2 = scores(hd)
            ta = _top_sorted(s1)
            tb = _top_sorted(s2)
            a_all = [jnp.where(sub == hd, ta[k], a_all[k]) for k in range(PEER_TOPK)]
            b_all = [jnp.where(sub == hd, tb[k], b_all[k]) for k in range(PEER_TOPK)]
        neg = jnp.full((SUBLANES, tm), NEG_BIG, F32)
        top = [a_all[0] + b_all[c] for c in range(PEER_TOPK)]
        for r in range(1, PEER_TOPK):
            n_r = PEER_TOPK // (r + 1)
            top = _merge_top(top, [a_all[r] + b_all[c] for c in range(n_r)] + [neg] * (PEER_TOPK - n_r))
        z = jnp.zeros((SUBLANES, tm), F32)
        for k in range(PEER_TOPK):
            z = z + jnp.exp(top[k] - top[0])
        thr = top[PEER_TOPK - 1]
        inv_z = 1.0 / z
        big = jnp.full((SUBLANES, tm), -NEG_BIG, F32)
        theta = []
        for k in range(PEER_TOPK):
            t = big
            for r in range(PEER_TOPK // (k + 1)):
                t = jnp.minimum(t, jnp.where(a_all[r] + b_all[k] >= thr, a_all[r], big))
            theta.append(t)
        for hd in range(PEER_HEADS):
            s1, s2 = scores(hd)
            tau = jnp.full(s1.shape, -NEG_BIG, F32)
            for k in range(PEER_TOPK):
                tau = jnp.where(s1 >= theta[k][hd:hd + 1, :], b_all[k][hd:hd + 1, :], tau)
            tau_ref[hd] = tau
            s2_ref[hd] = s2
            e1_ref[hd] = jnp.exp(s1 - a_all[0][hd:hd + 1, :]) * inv_z[hd:hd + 1, :]
            e2_ref[hd] = jnp.exp(s2 - b_all[0][hd:hd + 1, :])

    gs = []
    for i0 in range(0, rows_i, PEER_ROW_GROUP):
        gs += _routing_weights(e1_ref, tau_ref, e2_ref, s2_ref, step * rows_i + i0, PEER_ROW_GROUP)
    at = _dot(u_ref[...], ht_ref[...])
    wt = [(_gelu_tanh(at[ii * PEER_NKEYS:(ii + 1) * PEER_NKEYS, :]) * gs[ii]).astype(BF16)
          for ii in range(rows_i)]
    acc_ref[...] += _dot(vt_ref[0], jnp.concatenate(wt, axis=0))

    @pl.when(step == n_blocks - 1)
    def _():
        gate = jnp.where(is_ctx, mc_ref[5:6, :], mx_ref[0, 5:6, :])
        o_ref[0] = x_ref[0] + gate * acc_ref[...].T


def _tables_kernel(u_ref, v_ref, ub_ref, vt_ref):
    ub_ref[...] = u_ref[0].astype(BF16)
    vt_ref[0] = v_ref[0].T.astype(BF16)


def _peer_tables(u, v, layer):
    _, n_exp, d = u.shape
    eblk = PEER_EXPERT_BLOCK
    return pl.pallas_call(
        _tables_kernel,
        grid=(n_exp // eblk,),
        in_specs=[pl.BlockSpec((1, eblk, d), lambda e: (layer, e, 0)),
                  pl.BlockSpec((1, eblk, d), lambda e: (layer, e, 0))],
        out_specs=[pl.BlockSpec((eblk, d), lambda e: (e, 0)),
                   pl.BlockSpec((1, d, eblk), lambda e: (e, 0, 0))],
        out_shape=[jax.ShapeDtypeStruct((n_exp, d), BF16),
                   jax.ShapeDtypeStruct((n_exp // eblk, d, eblk), BF16)],
        name="peer_tables",
        compiler_params=pltpu.CompilerParams(
            dimension_semantics=("arbitrary",), vmem_limit_bytes=VMEM_LIMIT),
    )(u, v)


def _peer(xs, mx, mc, g, wqt, keys, u, vt, ctx_len, tm):
    b, l, d = xs.shape
    eblk = PEER_EXPERT_BLOCK
    n_blocks = u.shape[0] // eblk
    full = lambda shape: pl.BlockSpec(shape, lambda i, j, e: (0,) * len(shape))
    sc = (PEER_HEADS, PEER_NKEYS, tm)
    return pl.pallas_call(
        functools.partial(_peer_kernel, ctx_len=ctx_len, rows_i=eblk // PEER_NKEYS, n_blocks=n_blocks),
        grid=(b, l // tm, n_blocks),
        in_specs=[
            pl.BlockSpec((1, tm, d), lambda i, j, e: (i, j, 0)),
            pl.BlockSpec((1, 8, d), lambda i, j, e: (i, 0, 0)),
            full((8, d)), full((1, d)),
            pl.BlockSpec(wqt.shape, lambda i, j, e: (0, 0), pipeline_mode=pl.Buffered(1)),
            full(keys.shape),
            pl.BlockSpec((eblk, d), lambda i, j, e: (e, 0)),
            pl.BlockSpec((1, d, eblk), lambda i, j, e: (e, 0, 0)),
        ],
        out_specs=pl.BlockSpec((1, tm, d), lambda i, j, e: (i, j, 0)),
        out_shape=jax.ShapeDtypeStruct((b, l, d), F32),
        scratch_shapes=[
            pltpu.VMEM((d, tm), BF16), pltpu.VMEM((PEER_HEADS * PEER_DK, tm), BF16),
            pltpu.VMEM((d, tm), F32),
            pltpu.VMEM(sc, F32), pltpu.VMEM(sc, F32), pltpu.VMEM(sc, F32), pltpu.VMEM(sc, F32),
        ],
        name="peer",
        compiler_params=pltpu.CompilerParams(
            dimension_semantics=("parallel", "parallel", "arbitrary"), vmem_limit_bytes=VMEM_LIMIT),
    )(xs, mx, mc, g, wqt, keys, u, vt)


def kernel(x, c, ctx, c_ctx, w_mod, b_mod, g_mix, g_ffn, w_in_even, w_out_even, a_q_gain, a_k_gain,
           a_lambda, a_subln, b_w_af, b_b_af, b_w_ab, b_b_ab, b_gain, w_in_odd, w_out_odd, c_q_gain,
           c_k_gain, peer_wq, peer_keys, peer_u, peer_v):
    bsz, seq, d = x.shape
    n_ctx = ctx.shape[1]
    depth = w_mod.shape[0]
    n_ctx_tiles = n_ctx // TOK_TILE
    assert n_ctx % TOK_TILE == 0 and seq % TOK_TILE == 0 and (n_ctx + seq) % PEER_TOK_TILE == 0

    rows = seq // GRID_W
    row = jnp.repeat(jnp.arange(rows, dtype=F32), GRID_W)
    col = jnp.tile(jnp.arange(GRID_W, dtype=F32), rows)
    inv_freq = ROPE_THETA ** (-jnp.arange(ROPE_FREQS, dtype=F32) / ROPE_FREQS)
    ang = jnp.stack([row, col], axis=-1)[:, :, None] * inv_freq
    cos = jnp.cos(ang)
    sin = jnp.sin(ang)
    cos64 = jnp.concatenate([cos[:, 0], cos[:, 0], cos[:, 1], cos[:, 1]], axis=-1)
    sin64 = jnp.concatenate([-sin[:, 0], sin[:, 0], -sin[:, 1], sin[:, 1]], axis=-1)
    cos_t = jnp.concatenate([jnp.ones((n_ctx, LANES), F32), jnp.tile(cos64, (1, 2))], axis=0)
    sin_t = jnp.concatenate([jnp.zeros((n_ctx, LANES), F32), jnp.tile(sin64, (1, 2))], axis=0)
    seg = jnp.asarray(np.kron(np.eye(512 // HEAD_DIM), np.ones((HEAD_DIM, HEAD_DIM))), BF16)

    pad = (-(bsz + 1)) % SUBLANES
    cc = jnp.concatenate([c, c_ctx[None, :], jnp.zeros((pad, d), F32)], axis=0)
    mod = _modulation(cc, w_mod, b_mod)
    mod = mod.reshape(depth, bsz + 1 + pad, 6, d)
    mod = jnp.concatenate([mod, jnp.zeros((depth, bsz + 1 + pad, 2, d), F32)], axis=2)

    xs = jnp.concatenate([ctx, x], axis=1)
    scale = HEAD_DIM ** -0.5 * math.log2(math.e)
    for layer in range(depth):
        mx = mod[layer, :bsz]
        mc = mod[layer, bsz]
        i = layer // 2
        g1 = g_mix[layer][None, :]
        if layer % 2 == 0:
            lam_init = 0.8 - 0.6 * math.exp(-0.3 * layer)
            w = w_in_even[i]
            n_main = w.shape[1] - 2 * GATE_RANK
            gain = jnp.concatenate([jnp.tile(a_q_gain[i] * scale, 2 * A_HEADS),
                                    jnp.tile(a_k_gain[i], 2 * A_HEADS)])[None, :]
            wg = jnp.pad(w[:, n_main:], ((0, 0), (0, LANES - 2 * GATE_RANK))).astype(BF16)
            waf = jnp.pad(b_w_af[i], ((0, LANES - GATE_RANK), (0, 0))).astype(BF16)
            wab = jnp.pad(b_w_ab[i], ((GATE_RANK, LANES - 2 * GATE_RANK), (0, 0))).astype(BF16)
            qkv, gl = _inproj(xs, mx, mc, g1, w[:, :n_main].astype(BF16), gain, cos_t, sin_t, seg,
                              n_ctx_tiles, gain.shape[1],
                              gla_ops=(wg, waf, wab, b_b_af[i][None, :], b_b_ab[i][None, :]))
            dx = _diff_attn(qkv, a_lambda[i], a_subln[i][None, :], n_ctx_tiles, n_ctx, lam_init)
            o_f, o_b = _gla(gl, n_ctx_tiles)
            xs = _outproj(xs, mx, mc, w_out_even[i].astype(BF16), n_ctx_tiles, dx,
                          gla_ops=(o_f, o_b, gl, jnp.tile(b_gain[i], B_HEADS)[None, :]))
        else:
            w = w_in_odd[i]
            nq = C_Q_HEADS * HEAD_DIM
            nk = C_KV_HEADS * HEAD_DIM
            dup = lambda m: jnp.repeat(m.reshape(d, C_KV_HEADS, 1, HEAD_DIM), 2, axis=2).reshape(d, 2 * nk)
            w2 = jnp.concatenate([w[:, :nq], dup(w[:, nq:nq + nk]), dup(w[:, nq + nk:])], axis=1)
            gain = jnp.concatenate([jnp.tile(c_q_gain[i] * scale, C_Q_HEADS),
                                    jnp.tile(c_k_gain[i], 2 * C_KV_HEADS)])[None, :]
            qkv = _inproj(xs, mx, mc, g1, w2.astype(BF16), gain, cos_t, sin_t, seg,
                          n_ctx_tiles, gain.shape[1])[0]
            att = _gqa_attn(qkv, n_ctx_tiles, n_ctx)
            xs = _outproj(xs, mx, mc, w_out_odd[i].astype(BF16), n_ctx_tiles, att)
        u_bf, v_blocks = _peer_tables(peer_u, peer_v, layer)
        xs = _peer(xs, mx, mc, g_ffn[layer][None, :], peer_wq[layer].T.astype(BF16),
                   peer_keys[layer].astype(BF16), u_bf, v_blocks, n_ctx, PEER_TOK_TILE)
    return xs[:, n_ctx:, :]
```
